```python
import math
import jax, jax.numpy as jnp
from jax import lax
import numpy as np

D_MODEL = 1024
BATCH = 2
SEQ = 8192
DEPTH = 2
DEC_BATCH = 8
DEC_SEQ = 2048
PAST_LEN = 128

N_MIXERS = 2
N_HEADS = 16
HEAD_DIM = D_MODEL // N_HEADS
N_KV_HEADS = 4
Q_PER_KV = N_HEADS // N_KV_HEADS
ROT_HALF = HEAD_DIM // 2
ROPE_THETA = 10000.0
Q_BLOCK = 128
GRID_W = 64
SGU_CHUNK = 128
SGU_INNER = 2 * D_MODEL
SGU_GROUPS = 8
SGU_GROUP_DIM = SGU_INNER // SGU_GROUPS
FFN_DIM = 2816
CONV_W = 3
EPS = 1e-6

kernel_name = "hybrid_attn_sgu_convffn_encoder"


def rmsnorm(x, g):
    xf = x.astype(jnp.float32)
    y = xf * lax.rsqrt(jnp.mean(xf * xf, axis=-1, keepdims=True) + EPS)
    return (y * g.astype(jnp.float32)).astype(x.dtype)


def rotate_half(y):
    a, b = jnp.split(y, 2, axis=-1)
    return jnp.concatenate([-b, a], axis=-1)


def axial_rope_tables(rows):
    row_idx = jnp.broadcast_to(jnp.arange(rows)[:, None], (rows, GRID_W)).reshape(-1)
    col_idx = jnp.broadcast_to(jnp.arange(GRID_W)[None, :], (rows, GRID_W)).reshape(-1)
    inv_freq = ROPE_THETA ** (-jnp.arange(0, ROT_HALF, 2, dtype=jnp.float32) / ROT_HALF)
    ang_r = row_idx.astype(jnp.float32)[:, None] * inv_freq[None, :]
    ang_c = col_idx.astype(jnp.float32)[:, None] * inv_freq[None, :]
    ang = jnp.concatenate([ang_r, ang_r, ang_c, ang_c], axis=-1)
    return jnp.cos(ang), jnp.sin(ang)


def apply_axial_rope(x, cos, sin):
    xf = x.astype(jnp.float32)
    xr = jnp.concatenate([rotate_half(xf[..., :ROT_HALF]), rotate_half(xf[..., ROT_HALF:])], axis=-1)
    out = xf * cos[None, :, None, :] + xr * sin[None, :, None, :]
    return out.astype(x.dtype)


def attention_mixer(x, w_qkv, q_gain, k_gain, w_o):
    B, T, _ = x.shape
    rows = T // GRID_W
    cos, sin = axial_rope_tables(rows)
    qkv = x @ w_qkv
    q = qkv[..., :N_HEADS * HEAD_DIM].reshape(B, T, N_HEADS, HEAD_DIM)
    k = qkv[..., N_HEADS * HEAD_DIM:(N_HEADS + N_KV_HEADS) * HEAD_DIM].reshape(B, T, N_KV_HEADS, HEAD_DIM)
    v = qkv[..., (N_HEADS + N_KV_HEADS) * HEAD_DIM:].reshape(B, T, N_KV_HEADS, HEAD_DIM)
    q = apply_axial_rope(rmsnorm(q, q_gain), cos, sin)
    k = apply_axial_rope(rmsnorm(k, k_gain), cos, sin)
    scale = 1.0 / math.sqrt(HEAD_DIM)
    nb = T // Q_BLOCK
    qb = q.reshape(B, nb, Q_BLOCK, N_KV_HEADS, Q_PER_KV, HEAD_DIM).transpose(1, 0, 2, 3, 4, 5)

    def block(qi):
        s = jnp.einsum('bqkgd,bskd->bkgqs', qi, k).astype(jnp.float32) * scale
        p = jax.nn.softmax(s, axis=-1).astype(v.dtype)
        return jnp.einsum('bkgqs,bskd->bqkgd', p, v)

    o = lax.map(block, qb)
    o = o.transpose(1, 0, 2, 3, 4, 5).reshape(B, T, N_HEADS * HEAD_DIM)
    return o @ w_o


def sgu_mixer(x, w_in, v_gain, w_s, b_s, w_out):
    B, T, _ = x.shape
    nc = T // SGU_CHUNK
    z = jax.nn.gelu(x @ w_in)
    u, v = jnp.split(z, 2, axis=-1)
    v = rmsnorm(v, v_gain)
    vc = v.reshape(B, nc, SGU_CHUNK, SGU_GROUPS, SGU_GROUP_DIM)
    s = jnp.einsum('gpq,bcqgd->bcpgd', w_s, vc) + b_s.T[None, None, :, :, None]
    y = u * s.reshape(B, T, SGU_INNER)
    return y @ w_out


def conv_ffn(x, w_up, conv_w, conv_b, w_down):
    h = x @ w_up
    hp = jnp.pad(h, ((0, 0), (1, 1), (0, 0)))
    h = hp[:, :-2] * conv_w[0] + hp[:, 1:-1] * conv_w[1] + hp[:, 2:] * conv_w[2] + conv_b
    gate, up = jnp.split(h, 2, axis=-1)
    return (jax.nn.silu(gate) * up) @ w_down


def trunk(x, norm_mix, norm_ffn, attn_w_qkv, attn_q_norm, attn_k_norm, attn_w_o,
          sgu_w_in, sgu_v_norm, sgu_w_s, sgu_b_s, sgu_w_out,
          ffn_w_up, ffn_conv_w, ffn_conv_b, ffn_w_down):
    for i in range(DEPTH):
        h = rmsnorm(x, norm_mix[i])
        j = i // N_MIXERS
        if i % N_MIXERS == 0:
            mix = attention_mixer(h, attn_w_qkv[j], attn_q_norm[j], attn_k_norm[j], attn_w_o[j])
        else:
            mix = sgu_mixer(h, sgu_w_in[j], sgu_v_norm[j], sgu_w_s[j], sgu_b_s[j], sgu_w_out[j])
        x = x + mix
        h = rmsnorm(x, norm_ffn[i])
        x = x + conv_ffn(h, ffn_w_up[i], ffn_conv_w[i], ffn_conv_b[i], ffn_w_down[i])
    return x


def setup_inputs(seed: int = 0) -> dict:
    key = jax.random.key(seed)
    ks = jax.random.split(key, 20)
    n_a = (DEPTH + N_MIXERS - 1) // N_MIXERS
    n_b = DEPTH // N_MIXERS
    f32 = jnp.float32

    def w(k, shape, fan_in):
        return jax.random.normal(k, shape, f32) * (fan_in ** -0.5)

    def gain(k, shape):
        return 1.0 + 0.02 * jax.random.normal(k, shape, f32)

    qkv_out = (N_HEADS + 2 * N_KV_HEADS) * HEAD_DIM
    return {
        "x_prompt": jax.random.normal(ks[0], (BATCH, SEQ, D_MODEL), f32),
        "x_sample": jax.random.normal(ks[1], (DEC_BATCH, DEC_SEQ, D_MODEL), f32),
        "norm_mix": gain(ks[2], (DEPTH, D_MODEL)),
        "norm_ffn": gain(ks[3], (DEPTH, D_MODEL)),
        "attn_w_qkv": w(ks[4], (n_a, D_MODEL, qkv_out), D_MODEL),
        "attn_q_norm": gain(ks[5], (n_a, HEAD_DIM)),
        "attn_k_norm": gain(ks[6], (n_a, HEAD_DIM)),
        "attn_w_o": w(ks[7], (n_a, N_HEADS * HEAD_DIM, D_MODEL), N_HEADS * HEAD_DIM),
        "sgu_w_in": w(ks[8], (n_b, D_MODEL, 2 * SGU_INNER), D_MODEL),
        "sgu_v_norm": gain(ks[9], (n_b, SGU_INNER)),
        "sgu_w_s": w(ks[10], (n_b, SGU_GROUPS, SGU_CHUNK, SGU_CHUNK), SGU_CHUNK),
        "sgu_b_s": gain(ks[11], (n_b, SGU_GROUPS, SGU_CHUNK)),
        "sgu_w_out": w(ks[12], (n_b, SGU_INNER, D_MODEL), SGU_INNER),
        "ffn_w_up": w(ks[13], (DEPTH, D_MODEL, 2 * FFN_DIM), D_MODEL),
        "ffn_conv_w": w(ks[14], (DEPTH, CONV_W, 2 * FFN_DIM), CONV_W),
        "ffn_conv_b": 0.02 * jax.random.normal(ks[15], (DEPTH, 2 * FFN_DIM), f32),
        "ffn_w_down": w(ks[16], (DEPTH, FFN_DIM, D_MODEL), FFN_DIM),
    }


def reference(x_prompt, x_sample, norm_mix, norm_ffn, attn_w_qkv, attn_q_norm, attn_k_norm, attn_w_o,
              sgu_w_in, sgu_v_norm, sgu_w_s, sgu_b_s, sgu_w_out,
              ffn_w_up, ffn_conv_w, ffn_conv_b, ffn_w_down):
    y_prompt = trunk(x_prompt, norm_mix, norm_ffn, attn_w_qkv, attn_q_norm, attn_k_norm, attn_w_o,
                     sgu_w_in, sgu_v_norm, sgu_w_s, sgu_b_s, sgu_w_out,
                     ffn_w_up, ffn_conv_w, ffn_conv_b, ffn_w_down)
    y_sample = trunk(x_sample, norm_mix, norm_ffn, attn_w_qkv, attn_q_norm, attn_k_norm, attn_w_o,
                     sgu_w_in, sgu_v_norm, sgu_w_s, sgu_b_s, sgu_w_out,
                     ffn_w_up, ffn_conv_w, ffn_conv_b, ffn_w_down)
    return (y_prompt, y_sample)
```

```python
import functools
import math

import jax
import jax.numpy as jnp
from jax import lax
from jax.experimental import pallas as pl
from jax.experimental.pallas import tpu as pltpu

D_MODEL = 1024
N_HEADS = 16
HEAD_DIM = 64
N_KV_HEADS = 4
Q_PER_KV = N_HEADS // N_KV_HEADS
ROT_HALF = HEAD_DIM // 2
ROPE_THETA = 10000.0
GRID_W = 64
SGU_CHUNK = 128
SGU_INNER = 2 * D_MODEL
SGU_GROUPS = 8
SGU_GROUP_DIM = SGU_INNER // SGU_GROUPS
FFN_DIM = 2816
EPS = 1e-6

LANES = 128
BF16_SUBLANES = 16
VMEM_LIMIT_BYTES = 56 * 1024 * 1024
N_PAIRS = N_HEADS // 2
N_KV_PAIRS = N_KV_HEADS // 2
QKV_OUT = (N_HEADS + 2 * N_KV_HEADS) * HEAD_DIM
K_OFF = N_HEADS * HEAD_DIM
V_OFF = K_OFF + N_KV_HEADS * HEAD_DIM
FFN_CHUNK = 256
N_FFN_CHUNKS = FFN_DIM // FFN_CHUNK

F32 = jnp.float32
BF16 = jnp.bfloat16


def _cparams(*sem):
    return pltpu.CompilerParams(dimension_semantics=sem, vmem_limit_bytes=VMEM_LIMIT_BYTES)


def _const_spec(shape):
    nd = len(shape)
    return pl.BlockSpec(shape, lambda *_: (0,) * nd, pipeline_mode=pl.Buffered(1))


def _rmsnorm(x, g):
    y = x * lax.rsqrt(jnp.mean(x * x, axis=-1, keepdims=True) + EPS)
    return y * g


def _qkv_kernel(x_ref, g_ref, w_ref, gq_ref, gk_ref, cos_ref, sa_ref, sb_ref,
                qm_ref, k2_ref, v2_ref):
    x = x_ref[...]
    h = _rmsnorm(x, g_ref[...]).astype(BF16)
    qkv = jnp.dot(h, w_ref[...], preferred_element_type=F32)
    tm = x.shape[0]
    lane = lax.broadcasted_iota(jnp.int32, (tm, LANES), 1)
    lo = lane < HEAD_DIM
    cos = cos_ref[...]
    sa = sa_ref[...]
    sb = sb_ref[...]

    def norm_rope(blk, gain):
        sq = blk * blk
        ss_lo = jnp.sum(jnp.where(lo, sq, 0.0), axis=-1, keepdims=True)
        ss_hi = jnp.sum(jnp.where(lo, 0.0, sq), axis=-1, keepdims=True)
        r = jnp.where(lo, lax.rsqrt(ss_lo * (1.0 / HEAD_DIM) + EPS),
                      lax.rsqrt(ss_hi * (1.0 / HEAD_DIM) + EPS))
        xn = blk * r * gain
        return xn * cos + pltpu.roll(xn, 16, 1) * sa + pltpu.roll(xn, LANES - 16, 1) * sb

    gq = gq_ref[...]
    gk = gk_ref[...]
    for j in range(N_PAIRS):
        ro = norm_rope(qkv[:, LANES * j:LANES * (j + 1)], gq)
        qm_ref[2 * j] = jnp.where(lo, ro, 0.0).astype(BF16)
        qm_ref[2 * j + 1] = jnp.where(lo, 0.0, ro).astype(BF16)
    for j in range(N_KV_PAIRS):
        ro = norm_rope(qkv[:, K_OFF + LANES * j:K_OFF + LANES * (j + 1)], gk)
        sw = pltpu.roll(ro, HEAD_DIM, 1)
        k2_ref[2 * j] = jnp.where(lo, ro, sw).astype(BF16)
        k2_ref[2 * j + 1] = jnp.where(lo, sw, ro).astype(BF16)
        vv = qkv[:, V_OFF + LANES * j:V_OFF + LANES * (j + 1)]
        sw = pltpu.roll(vv, HEAD_DIM, 1)
        v2_ref[2 * j] = jnp.where(lo, vv, sw).astype(BF16)
        v2_ref[2 * j + 1] = jnp.where(lo, sw, vv).astype(BF16)


def _qkv_call(x, g, w, gq, gk, cos, sa, sb, seq, tm):
    T = x.shape[0]
    tiles_per_seq = seq // tm
    tab_spec = pl.BlockSpec((tm, LANES), lambda i: (i % tiles_per_seq, 0))
    return pl.pallas_call(
        _qkv_kernel,
        grid=(T // tm,),
        in_specs=[
            pl.BlockSpec((tm, D_MODEL), lambda i: (i, 0)),
            _const_spec((1, D_MODEL)),
            _const_spec((D_MODEL, QKV_OUT)),
            _const_spec((1, LANES)),
            _const_spec((1, LANES)),
            tab_spec, tab_spec, tab_spec,
        ],
        out_specs=[
            pl.BlockSpec((N_HEADS, tm, LANES), lambda i: (0, i, 0)),
            pl.BlockSpec((N_KV_HEADS, tm, LANES), lambda i: (0, i, 0)),
            pl.BlockSpec((N_KV_HEADS, tm, LANES), lambda i: (0, i, 0)),
        ],
        out_shape=[
            jax.ShapeDtypeStruct((N_HEADS, T, LANES), BF16),
            jax.ShapeDtypeStruct((N_KV_HEADS, T, LANES), BF16),
            jax.ShapeDtypeStruct((N_KV_HEADS, T, LANES), BF16),
        ],
        compiler_params=_cparams("parallel"),
        name="qkv",
    )(x, g, w, gq, gk, cos, sa, sb)


def _attn_kernel(q_ref, k_ref, v_ref, o_ref, m_ref, l_ref, acc_ref):
    j = pl.program_id(2)
    nk = pl.num_programs(2)
    tq = q_ref.shape[1]
    tk = k_ref.shape[1]

    @pl.when(j == 0)
    def _():
        m_ref[...] = jnp.full(m_ref.shape, -jnp.inf, F32)
        l_ref[...] = jnp.zeros(l_ref.shape, F32)
        acc_ref[...] = jnp.zeros(acc_ref.shape, F32)

    def head(h, carry):
        g = h // Q_PER_KV
        q = q_ref[h]
        k = k_ref[g]
        v = v_ref[g]
        s = lax.dot_general(q, k, (((1,), (1,)), ((), ())), preferred_element_type=F32)
        m_prev = m_ref[h]
        m_cur = s[:, :LANES]
        for c in range(1, tk // LANES):
            m_cur = jnp.maximum(m_cur, s[:, c * LANES:(c + 1) * LANES])
        m_next = jnp.maximum(m_prev, jnp.max(m_cur, axis=1, keepdims=True))
        alpha = jnp.exp(m_prev - m_next)
        ps = []
        l_part = None
        for c in range(tk // LANES):
            p = jnp.exp(s[:, c * LANES:(c + 1) * LANES] - m_next)
            l_part = p if l_part is None else l_part + p
            ps.append(p.astype(BF16))
        p = jnp.concatenate(ps, axis=1)
        pv = jnp.dot(p, v, preferred_element_type=F32)
        m_ref[h] = m_next
        l_ref[h] = alpha * l_ref[h] + l_part
        acc_ref[h] = alpha * acc_ref[h] + pv
        return carry

    lax.fori_loop(0, N_HEADS, head, 0)

    @pl.when(j == nk - 1)
    def _():
        lane = lax.broadcasted_iota(jnp.int32, (tq, LANES), 1)
        lo = lane < HEAD_DIM
        for pr in range(N_PAIRS):
            o0 = acc_ref[2 * pr] / jnp.sum(l_ref[2 * pr], axis=1, keepdims=True)
            o1 = acc_ref[2 * pr + 1] / jnp.sum(l_ref[2 * pr + 1], axis=1, keepdims=True)
            o_ref[pr] = jnp.where(lo, o0, o1).astype(BF16)


def _attn_call(qm, k2, v2, batch, seq, tq, tk):
    T = qm.shape[1]
    nq = seq // tq
    nk = seq // tk
    return pl.pallas_call(
        _attn_kernel,
        grid=(batch, nq, nk),
        in_specs=[
            pl.BlockSpec((N_HEADS, tq, LANES), lambda b, i, j: (0, b * nq + i, 0)),
            pl.BlockSpec((N_KV_HEADS, tk, LANES), lambda b, i, j: (0, b * nk + j, 0)),
            pl.BlockSpec((N_KV_HEADS, tk, LANES), lambda b, i, j: (0, b * nk + j, 0)),
        ],
        out_specs=pl.BlockSpec((N_PAIRS, tq, LANES), lambda b, i, j: (0, b * nq + i, 0)),
        out_shape=jax.ShapeDtypeStruct((N_PAIRS, T, LANES), BF16),
        scratch_shapes=[
            pltpu.VMEM((N_HEADS, tq, LANES), F32),
            pltpu.VMEM((N_HEADS, tq, LANES), F32),
            pltpu.VMEM((N_HEADS, tq, LANES), F32),
        ],
        compiler_params=_cparams("parallel", "parallel", "arbitrary"),
        name="attn",
    )(qm, k2, v2)


def _oproj_kernel(o_ref, x_ref, w_ref, g_ref, xo_ref, h_ref):
    o = jnp.concatenate([o_ref[pr] for pr in range(N_PAIRS)], axis=1)
    x1 = x_ref[...] + jnp.dot(o, w_ref[...], preferred_element_type=F32)
    xo_ref[...] = x1
    h_ref[...] = _rmsnorm(x1, g_ref[...]).astype(BF16)


def _oproj_call(o, x, w, g, tm):
    T = x.shape[0]
    return pl.pallas_call(
        _oproj_kernel,
        grid=(T // tm,),
        in_specs=[
            pl.BlockSpec((N_PAIRS, tm, LANES), lambda i: (0, i, 0)),
            pl.BlockSpec((tm, D_MODEL), lambda i: (i, 0)),
            _const_spec((D_MODEL, D_MODEL)),
            _const_spec((1, D_MODEL)),
        ],
        out_specs=[
            pl.BlockSpec((tm, D_MODEL), lambda i: (i, 0)),
            pl.BlockSpec((tm, D_MODEL), lambda i: (i, 0)),
        ],
        out_shape=[
            jax.ShapeDtypeStruct((T, D_MODEL), F32),
            jax.ShapeDtypeStruct((T, D_MODEL), BF16),
        ],
        compiler_params=_cparams("parallel"),
        name="oproj",
    )(o, x, w, g)


def _ffn_kernel(tiles_per_seq, with_norm, hp_ref, h_ref, hn_ref, x_ref, wup_ref, cw_ref, cb_ref,
                wdn_ref, g_ref, xo_ref, *rest):
    i = pl.program_id(0)
    tm = h_ref.shape[0]
    halo = hp_ref.shape[0]
    pos = i % tiles_per_seq
    hp = hp_ref[...]
    hn = hn_ref[...]
    hp = jnp.where(pos == 0, jnp.zeros_like(hp), hp)
    hn = jnp.where(pos == tiles_per_seq - 1, jnp.zeros_like(hn), hn)
    hcat = jnp.concatenate([hp, h_ref[...], hn], axis=0)
    rows = tm + 2 * halo

    def conv(c0):
        y = jnp.dot(hcat, wup_ref[:, c0:c0 + FFN_CHUNK], preferred_element_type=F32)
        prev = pltpu.roll(y, 1, 0)[halo:halo + tm]
        nxt = pltpu.roll(y, rows - 1, 0)[halo:halo + tm]
        w = cw_ref[:, c0:c0 + FFN_CHUNK]
        return (prev * w[0:1] + y[halo:halo + tm] * w[1:2] + nxt * w[2:3]
                + cb_ref[:, c0:c0 + FFN_CHUNK])

    acc = jnp.zeros((tm, D_MODEL), F32)
    for c in range(N_FFN_CHUNKS):
        gate = conv(c * FFN_CHUNK)
        up = conv(FFN_DIM + c * FFN_CHUNK)
        act = (gate * (1.0 / (1.0 + jnp.exp(-gate))) * up).astype(BF16)
        acc = acc + jnp.dot(act, wdn_ref[c * FFN_CHUNK:(c + 1) * FFN_CHUNK, :],
                            preferred_element_type=F32)
    x2 = x_ref[...] + acc
    xo_ref[...] = x2
    if with_norm:
        rest[0][...] = _rmsnorm(x2, g_ref[...]).astype(BF16)


def _ffn_call(h, x, wup, cw, cb, wdn, g, seq, tm, with_norm):
    T = x.shape[0]
    halo = BF16_SUBLANES
    nblk = tm // halo
    last_blk = T // halo - 1
    out_specs = [pl.BlockSpec((tm, D_MODEL), lambda i: (i, 0))]
    out_shape = [jax.ShapeDtypeStruct((T, D_MODEL), F32)]
    if with_norm:
        out_specs.append(pl.BlockSpec((tm, D_MODEL), lambda i: (i, 0)))
        out_shape.append(jax.ShapeDtypeStruct((T, D_MODEL), BF16))
    res = pl.pallas_call(
        functools.partial(_ffn_kernel, seq // tm, with_norm),
        grid=(T // tm,),
        in_specs=[
            pl.BlockSpec((halo, D_MODEL), lambda i: (jnp.maximum(i * nblk - 1, 0), 0)),
            pl.BlockSpec((tm, D_MODEL), lambda i: (i, 0)),
            pl.BlockSpec((halo, D_MODEL), lambda i: (jnp.minimum((i + 1) * nblk, last_blk), 0)),
            pl.BlockSpec((tm, D_MODEL), lambda i: (i, 0)),
            _const_spec((D_MODEL, 2 * FFN_DIM)),
            _const_spec((3, 2 * FFN_DIM)),
            _const_spec((1, 2 * FFN_DIM)),
            _const_spec((FFN_DIM, D_MODEL)),
            _const_spec((1, D_MODEL)),
        ],
        out_specs=out_specs,
        out_shape=out_shape,
        compiler_params=_cparams("parallel"),
        name="ffn",
    )(h, h, h, x, wup, cw, cb, wdn, g)
    return res if with_norm else (res[0], None)


def _gelu(x):
    c = math.sqrt(2.0 / math.pi)
    return 0.5 * x * (1.0 + jnp.tanh(c * (x + 0.044715 * (x * x * x))))


def _sgu_kernel(h_ref, x_ref, win_ref, vg_ref, ws_ref, bs_ref, wout_ref, g_ref, xo_ref, ho_ref):
    tm = h_ref.shape[0]
    h = h_ref[...]
    v = _gelu(jnp.dot(h, win_ref[:, SGU_INNER:], preferred_element_type=F32))
    v = _rmsnorm(v, vg_ref[...]).astype(BF16)
    acc = jnp.zeros((tm, D_MODEL), F32)
    for g in range(SGU_GROUPS):
        c0 = g * SGU_GROUP_DIM
        u = _gelu(jnp.dot(h, win_ref[:, c0:c0 + SGU_GROUP_DIM], preferred_element_type=F32))
        ws = ws_ref[g]
        bias = bs_ref[:, c0:c0 + SGU_GROUP_DIM]
        ss = []
        for c in range(tm // SGU_CHUNK):
            vc = v[c * SGU_CHUNK:(c + 1) * SGU_CHUNK, c0:c0 + SGU_GROUP_DIM]
            ss.append(jnp.dot(ws, vc, preferred_element_type=F32) + bias)
        y = (u * jnp.concatenate(ss, axis=0)).astype(BF16)
        acc = acc + jnp.dot(y, wout_ref[c0:c0 + SGU_GROUP_DIM, :], preferred_element_type=F32)
    x2 = x_ref[...] + acc
    xo_ref[...] = x2
    ho_ref[...] = _rmsnorm(x2, g_ref[...]).astype(BF16)


def _sgu_call(h, x, win, vg, ws, bs, wout, g, tm):
    T = x.shape[0]
    return pl.pallas_call(
        _sgu_kernel,
        grid=(T // tm,),
        in_specs=[
            pl.BlockSpec((tm, D_MODEL), lambda i: (i, 0)),
            pl.BlockSpec((tm, D_MODEL), lambda i: (i, 0)),
            _const_spec((D_MODEL, 2 * SGU_INNER)),
            _const_spec((1, SGU_INNER)),
            _const_spec((SGU_GROUPS, SGU_CHUNK, SGU_CHUNK)),
            _const_spec((SGU_CHUNK, SGU_INNER)),
            _const_spec((SGU_INNER, D_MODEL)),
            _const_spec((1, D_MODEL)),
        ],
        out_specs=[
            pl.BlockSpec((tm, D_MODEL), lambda i: (i, 0)),
            pl.BlockSpec((tm, D_MODEL), lambda i: (i, 0)),
        ],
        out_shape=[
            jax.ShapeDtypeStruct((T, D_MODEL), F32),
            jax.ShapeDtypeStruct((T, D_MODEL), BF16),
        ],
        compiler_params=_cparams("parallel"),
        name="sgu",
    )(h, x, win, vg, ws, bs, wout, g)


def _rope_tables(seq):
    rows = seq // GRID_W
    row_idx = jnp.broadcast_to(jnp.arange(rows)[:, None], (rows, GRID_W)).reshape(-1)
    col_idx = jnp.broadcast_to(jnp.arange(GRID_W)[None, :], (rows, GRID_W)).reshape(-1)
    inv_freq = ROPE_THETA ** (-jnp.arange(0, ROT_HALF, 2, dtype=F32) / ROT_HALF)
    ang_r = row_idx.astype(F32)[:, None] * inv_freq[None, :]
    ang_c = col_idx.astype(F32)[:, None] * inv_freq[None, :]
    ang = jnp.concatenate([ang_r, ang_r, ang_c, ang_c], axis=-1)
    cos = jnp.tile(jnp.cos(ang), (1, 2))
    sin = jnp.tile(jnp.sin(ang), (1, 2))
    second = (jnp.arange(LANES) % ROT_HALF) >= (ROT_HALF // 2)
    sa = jnp.where(second[None, :], sin, 0.0)
    sb = jnp.where(second[None, :], 0.0, -sin)
    return cos, sa, sb


def _trunk(x3, p, tq, tk):
    batch, seq, _ = x3.shape
    x = x3.reshape(batch * seq, D_MODEL)
    tm = 512
    cos, sa, sb = _rope_tables(seq)
    qm, k2, v2 = _qkv_call(x, p["g_mix0"], p["w_qkv"], p["gq"], p["gk"], cos, sa, sb, seq, tm)
    o = _attn_call(qm, k2, v2, batch, seq, tq, tk)
    x, h = _oproj_call(o, x, p["w_o"], p["g_ffn0"], tm)
    x, h = _ffn_call(h, x, p["w_up0"], p["cw0"], p["cb0"], p["w_dn0"], p["g_mix1"], seq, tm, True)
    x, h = _sgu_call(h, x, p["w_in"], p["vg"], p["w_s"], p["b_s"], p["w_out"], p["g_ffn1"], 256)
    x, _ = _ffn_call(h, x, p["w_up1"], p["cw1"], p["cb1"], p["w_dn1"], p["g_ffn1"], seq, tm, False)
    return x.reshape(batch, seq, D_MODEL)


def kernel(x_prompt, x_sample, norm_mix, norm_ffn, attn_w_qkv, attn_q_norm, attn_k_norm, attn_w_o,
           sgu_w_in, sgu_v_norm, sgu_w_s, sgu_b_s, sgu_w_out,
           ffn_w_up, ffn_conv_w, ffn_conv_b, ffn_w_down):
    scale = 1.0 / math.sqrt(HEAD_DIM)
    p = {
        "g_mix0": norm_mix[0][None, :], "g_mix1": norm_mix[1][None, :],
        "g_ffn0": norm_ffn[0][None, :], "g_ffn1": norm_ffn[1][None, :],
        "w_qkv": attn_w_qkv[0].astype(BF16),
        "gq": jnp.tile(attn_q_norm[0] * scale, 2)[None, :],
        "gk": jnp.tile(attn_k_norm[0], 2)[None, :],
        "w_o": attn_w_o[0].astype(BF16),
        "w_in": sgu_w_in[0].astype(BF16),
        "vg": sgu_v_norm[0][None, :],
        "w_s": sgu_w_s[0].astype(BF16),
        "b_s": jnp.repeat(sgu_b_s[0].T, SGU_GROUP_DIM, axis=1),
        "w_out": sgu_w_out[0].astype(BF16),
        "w_up0": ffn_w_up[0].astype(BF16), "w_up1": ffn_w_up[1].astype(BF16),
        "cw0": ffn_conv_w[0], "cw1": ffn_conv_w[1],
        "cb0": ffn_conv_b[0][None, :], "cb1": ffn_conv_b[1][None, :],
        "w_dn0": ffn_w_down[0].astype(BF16), "w_dn1": ffn_w_down[1].astype(BF16),
    }
    y_prompt = _trunk(x_prompt, p, 512, 512)
    y_sample = _trunk(x_sample, p, 512, 512)
    return (y_prompt, y_sample)
```

```python
import functools
import math

import jax
import jax.numpy as jnp
from jax import lax
from jax.experimental import pallas as pl
from jax.experimental.pallas import tpu as pltpu

D_MODEL = 1024
N_HEADS = 16
HEAD_DIM = 64
N_KV_HEADS = 4
Q_PER_KV = N_HEADS // N_KV_HEADS
ROT_HALF = HEAD_DIM // 2
ROPE_THETA = 10000.0
GRID_W = 64
SGU_CHUNK = 128
SGU_INNER = 2 * D_MODEL
SGU_GROUPS = 8
SGU_GROUP_DIM = SGU_INNER // SGU_GROUPS
FFN_DIM = 2816
EPS = 1e-6
LOG2_E = math.log2(math.e)

LANES = 128
BF16_SUBLANES = 16
VMEM_LIMIT_BYTES = 56 * 1024 * 1024
N_PAIRS = N_HEADS // 2
N_KV_PAIRS = N_KV_HEADS // 2
QKV_OUT = (N_HEADS + 2 * N_KV_HEADS) * HEAD_DIM
K_OFF = N_HEADS * HEAD_DIM
V_OFF = K_OFF + N_KV_HEADS * HEAD_DIM
FFN_CHUNK = 256
N_FFN_CHUNKS = FFN_DIM // FFN_CHUNK

F32 = jnp.float32
BF16 = jnp.bfloat16


def _cparams(*sem):
    return pltpu.CompilerParams(dimension_semantics=sem, vmem_limit_bytes=VMEM_LIMIT_BYTES)


def _const_spec(shape):
    nd = len(shape)
    return pl.BlockSpec(shape, lambda *_: (0,) * nd, pipeline_mode=pl.Buffered(1))


def _rmsnorm(x, g):
    y = x * lax.rsqrt(jnp.mean(x * x, axis=-1, keepdims=True) + EPS)
    return y * g


def _qkv_kernel(x_ref, g_ref, w_ref, gq_ref, gk_ref, cos_ref, sa_ref, sb_ref,
                qm_ref, k2_ref, v2_ref):
    x = x_ref[...]
    h = _rmsnorm(x, g_ref[...]).astype(BF16)
    qkv = jnp.dot(h, w_ref[...], preferred_element_type=F32)
    tm = x.shape[0]
    lane = lax.broadcasted_iota(jnp.int32, (tm, LANES), 1)
    lo = lane < HEAD_DIM
    cos = cos_ref[...]
    sa = sa_ref[...]
    sb = sb_ref[...]

    def norm_rope(blk, gain):
        sq = blk * blk
        ss_lo = jnp.sum(jnp.where(lo, sq, 0.0), axis=-1, keepdims=True)
        ss_hi = jnp.sum(jnp.where(lo, 0.0, sq), axis=-1, keepdims=True)
        r = jnp.where(lo, lax.rsqrt(ss_lo * (1.0 / HEAD_DIM) + EPS),
                      lax.rsqrt(ss_hi * (1.0 / HEAD_DIM) + EPS))
        xn = blk * r * gain
        return xn * cos + pltpu.roll(xn, 16, 1) * sa + pltpu.roll(xn, LANES - 16, 1) * sb

    gq = gq_ref[...]
    gk = gk_ref[...]
    for j in range(N_PAIRS):
        ro = norm_rope(qkv[:, LANES * j:LANES * (j + 1)], gq)
        qm_ref[2 * j] = jnp.where(lo, ro, 0.0).astype(BF16)
        qm_ref[2 * j + 1] = jnp.where(lo, 0.0, ro).astype(BF16)
    for j in range(N_KV_PAIRS):
        ro = norm_rope(qkv[:, K_OFF + LANES * j:K_OFF + LANES * (j + 1)], gk)
        sw = pltpu.roll(ro, HEAD_DIM, 1)
        k2_ref[2 * j] = jnp.where(lo, ro, sw).astype(BF16)
        k2_ref[2 * j + 1] = jnp.where(lo, sw, ro).astype(BF16)
        vv = qkv[:, V_OFF + LANES * j:V_OFF + LANES * (j + 1)]
        sw = pltpu.roll(vv, HEAD_DIM, 1)
        v2_ref[4 * j] = jnp.where(lo, vv, 1.0).astype(BF16)
        v2_ref[4 * j + 1] = jnp.where(lo, 1.0, sw).astype(BF16)
        v2_ref[4 * j + 2] = jnp.where(lo, sw, 1.0).astype(BF16)
        v2_ref[4 * j + 3] = jnp.where(lo, 1.0, vv).astype(BF16)


def _qkv_call(x, g, w, gq, gk, cos, sa, sb, seq, tm):
    T = x.shape[0]
    tiles_per_seq = seq // tm
    tab_spec = pl.BlockSpec((tm, LANES), lambda i: (i % tiles_per_seq, 0))
    return pl.pallas_call(
        _qkv_kernel,
        grid=(T // tm,),
        in_specs=[
            pl.BlockSpec((tm, D_MODEL), lambda i: (i, 0)),
            _const_spec((1, D_MODEL)),
            _const_spec((D_MODEL, QKV_OUT)),
            _const_spec((1, LANES)),
            _const_spec((1, LANES)),
            tab_spec, tab_spec, tab_spec,
        ],
        out_specs=[
            pl.BlockSpec((N_HEADS, tm, LANES), lambda i: (0, i, 0)),
            pl.BlockSpec((N_KV_HEADS, tm, LANES), lambda i: (0, i, 0)),
            pl.BlockSpec((2 * N_KV_HEADS, tm, LANES), lambda i: (0, i, 0)),
        ],
        out_shape=[
            jax.ShapeDtypeStruct((N_HEADS, T, LANES), BF16),
            jax.ShapeDtypeStruct((N_KV_HEADS, T, LANES), BF16),
            jax.ShapeDtypeStruct((2 * N_KV_HEADS, T, LANES), BF16),
        ],
        compiler_params=_cparams("parallel"),
        name="qkv",
    )(x, g, w, gq, gk, cos, sa, sb)


def _attn_kernel(q_ref, k_ref, v_ref, o_ref, m_ref, acc_ref):
    j = pl.program_id(2)
    nk = pl.num_programs(2)
    tq = q_ref.shape[1]
    tk = k_ref.shape[1]

    @pl.when(j == 0)
    def _():
        m_ref[...] = jnp.full(m_ref.shape, -jnp.inf, F32)
        acc_ref[...] = jnp.zeros(acc_ref.shape, F32)

    def scores(h):
        return lax.dot_general(q_ref[h], k_ref[h // Q_PER_KV], (((1,), (1,)), ((), ())),
                               preferred_element_type=F32)

    def softmax_pv(h, s):
        m_prev = m_ref[h]
        m_cur = s[:, :LANES]
        for c in range(1, tk // LANES):
            m_cur = jnp.maximum(m_cur, s[:, c * LANES:(c + 1) * LANES])
        m_next = jnp.maximum(m_prev, jnp.max(m_cur, axis=1, keepdims=True))
        alpha = jnp.exp2(m_prev - m_next)
        p = jnp.concatenate(
            [jnp.exp2(s[:, c * LANES:(c + 1) * LANES] - m_next).astype(BF16)
             for c in range(tk // LANES)], axis=1)
        v = v_ref[2 * (h // Q_PER_KV) + (h % 2)]
        m_ref[h] = m_next
        acc_ref[h] = alpha * acc_ref[h] + jnp.dot(p, v, preferred_element_type=F32)

    s_cur = scores(0)
    for h in range(N_HEADS):
        s_next = scores(h + 1) if h + 1 < N_HEADS else None
        softmax_pv(h, s_cur)
        s_cur = s_next

    @pl.when(j == nk - 1)
    def _():
        lane = lax.broadcasted_iota(jnp.int32, (tq, LANES), 1)
        lo = lane < HEAD_DIM
        for pr in range(N_PAIRS):
            a0 = acc_ref[2 * pr]
            a1 = acc_ref[2 * pr + 1]
            o0 = a0 / pltpu.roll(a0, HEAD_DIM, 1)
            o1 = a1 / pltpu.roll(a1, HEAD_DIM, 1)
            o_ref[pr] = jnp.where(lo, o0, o1).astype(BF16)


def _attn_call(qm, k2, v2, batch, seq, tq, tk):
    T = qm.shape[1]
    nq = seq // tq
    nk = seq // tk
    return pl.pallas_call(
        _attn_kernel,
        grid=(batch, nq, nk),
        in_specs=[
            pl.BlockSpec((N_HEADS, tq, LANES), lambda b, i, j: (0, b * nq + i, 0)),
            pl.BlockSpec((N_KV_HEADS, tk, LANES), lambda b, i, j: (0, b * nk + j, 0)),
            pl.BlockSpec((2 * N_KV_HEADS, tk, LANES), lambda b, i, j: (0, b * nk + j, 0)),
        ],
        out_specs=pl.BlockSpec((N_PAIRS, tq, LANES), lambda b, i, j: (0, b * nq + i, 0)),
        out_shape=jax.ShapeDtypeStruct((N_PAIRS, T, LANES), BF16),
        scratch_shapes=[
            pltpu.VMEM((N_HEADS, tq, LANES), F32),
            pltpu.VMEM((N_HEADS, tq, LANES), F32),
        ],
        compiler_params=_cparams("parallel", "parallel", "arbitrary"),
        name="attn",
    )(qm, k2, v2)


def _oproj_kernel(o_ref, x_ref, w_ref, g_ref, xo_ref, h_ref):
    o = jnp.concatenate([o_ref[pr] for pr in range(N_PAIRS)], axis=1)
    x1 = x_ref[...] + jnp.dot(o, w_ref[...], preferred_element_type=F32)
    xo_ref[...] = x1
    h_ref[...] = _rmsnorm(x1, g_ref[...]).astype(BF16)


def _oproj_call(o, x, w, g, tm):
    T = x.shape[0]
    return pl.pallas_call(
        _oproj_kernel,
        grid=(T // tm,),
        in_specs=[
            pl.BlockSpec((N_PAIRS, tm, LANES), lambda i: (0, i, 0)),
            pl.BlockSpec((tm, D_MODEL), lambda i: (i, 0)),
            _const_spec((D_MODEL, D_MODEL)),
            _const_spec((1, D_MODEL)),
        ],
        out_specs=[
            pl.BlockSpec((tm, D_MODEL), lambda i: (i, 0)),
            pl.BlockSpec((tm, D_MODEL), lambda i: (i, 0)),
        ],
        out_shape=[
            jax.ShapeDtypeStruct((T, D_MODEL), F32),
            jax.ShapeDtypeStruct((T, D_MODEL), BF16),
        ],
        compiler_params=_cparams("parallel"),
        name="oproj",
    )(o, x, w, g)


def _ffn_kernel(tiles_per_seq, with_norm, hp_ref, h_ref, hn_ref, x_ref, wup_ref, cw_ref, cb_ref,
                wdn_ref, g_ref, xo_ref, *rest):
    i = pl.program_id(0)
    tm = h_ref.shape[0]
    halo = hp_ref.shape[0]
    pos = i % tiles_per_seq
    hp = hp_ref[...]
    hn = hn_ref[...]
    hp = jnp.where(pos == 0, jnp.zeros_like(hp), hp)
    hn = jnp.where(pos == tiles_per_seq - 1, jnp.zeros_like(hn), hn)
    hcat = jnp.concatenate([hp, h_ref[...], hn], axis=0)
    rows = tm + 2 * halo

    def up_proj(c0):
        return jnp.dot(hcat, wup_ref[:, c0:c0 + FFN_CHUNK], preferred_element_type=F32)

    def conv(y, c0):
        prev = pltpu.roll(y, 1, 0)[halo:halo + tm]
        nxt = pltpu.roll(y, rows - 1, 0)[halo:halo + tm]
        w = cw_ref[:, c0:c0 + FFN_CHUNK]
        return (prev * w[0:1] + y[halo:halo + tm] * w[1:2] + nxt * w[2:3]
                + cb_ref[:, c0:c0 + FFN_CHUNK])

    acc = jnp.zeros((tm, D_MODEL), F32)
    y_gate = up_proj(0)
    y_up = up_proj(FFN_DIM)
    for c in range(N_FFN_CHUNKS):
        if c + 1 < N_FFN_CHUNKS:
            y_gate_next = up_proj((c + 1) * FFN_CHUNK)
            y_up_next = up_proj(FFN_DIM + (c + 1) * FFN_CHUNK)
        gate = conv(y_gate, c * FFN_CHUNK)
        up = conv(y_up, FFN_DIM + c * FFN_CHUNK)
        y_gate, y_up = y_gate_next, y_up_next
        act = (gate * (1.0 / (1.0 + jnp.exp(-gate))) * up).astype(BF16)
        acc = acc + jnp.dot(act, wdn_ref[c * FFN_CHUNK:(c + 1) * FFN_CHUNK, :],
                            preferred_element_type=F32)
    x2 = x_ref[...] + acc
    xo_ref[...] = x2
    if with_norm:
        rest[0][...] = _rmsnorm(x2, g_ref[...]).astype(BF16)


def _ffn_call(h, x, wup, cw, cb, wdn, g, seq, tm, with_norm):
    T = x.shape[0]
    halo = BF16_SUBLANES
    nblk = tm // halo
    last_blk = T // halo - 1
    out_specs = [pl.BlockSpec((tm, D_MODEL), lambda i: (i, 0))]
    out_shape = [jax.ShapeDtypeStruct((T, D_MODEL), F32)]
    if with_norm:
        out_specs.append(pl.BlockSpec((tm, D_MODEL), lambda i: (i, 0)))
        out_shape.append(jax.ShapeDtypeStruct((T, D_MODEL), BF16))
    res = pl.pallas_call(
        functools.partial(_ffn_kernel, seq // tm, with_norm),
        grid=(T // tm,),
        in_specs=[
            pl.BlockSpec((halo, D_MODEL), lambda i: (jnp.maximum(i * nblk - 1, 0), 0)),
            pl.BlockSpec((tm, D_MODEL), lambda i: (i, 0)),
            pl.BlockSpec((halo, D_MODEL), lambda i: (jnp.minimum((i + 1) * nblk, last_blk), 0)),
            pl.BlockSpec((tm, D_MODEL), lambda i: (i, 0)),
            _const_spec((D_MODEL, 2 * FFN_DIM)),
            _const_spec((3, 2 * FFN_DIM)),
            _const_spec((1, 2 * FFN_DIM)),
            _const_spec((FFN_DIM, D_MODEL)),
            _const_spec((1, D_MODEL)),
        ],
        out_specs=out_specs,
        out_shape=out_shape,
        compiler_params=_cparams("parallel"),
        name="ffn",
    )(h, h, h, x, wup, cw, cb, wdn, g)
    return res if with_norm else (res[0], None)


def _gelu(x):
    c = math.sqrt(2.0 / math.pi)
    return 0.5 * x * (1.0 + jnp.tanh(c * (x + 0.044715 * (x * x * x))))


def _sgu_kernel(h_ref, x_ref, win_ref, vg_ref, ws_ref, bs_ref, wout_ref, g_ref, xo_ref, ho_ref):
    tm = h_ref.shape[0]
    h = h_ref[...]
    v = _gelu(jnp.dot(h, win_ref[:, SGU_INNER:], preferred_element_type=F32))
    v = _rmsnorm(v, vg_ref[...]).astype(BF16)
    acc = jnp.zeros((tm, D_MODEL), F32)

    def u_proj(g):
        return jnp.dot(h, win_ref[:, g * SGU_GROUP_DIM:(g + 1) * SGU_GROUP_DIM],
                       preferred_element_type=F32)

    u_pre = u_proj(0)
    for g in range(SGU_GROUPS):
        c0 = g * SGU_GROUP_DIM
        u_pre_next = u_proj(g + 1) if g + 1 < SGU_GROUPS else None
        u = _gelu(u_pre)
        u_pre = u_pre_next
        ws = ws_ref[g]
        bias = bs_ref[:, c0:c0 + SGU_GROUP_DIM]
        ss = []
        for c in range(tm // SGU_CHUNK):
            vc = v[c * SGU_CHUNK:(c + 1) * SGU_CHUNK, c0:c0 + SGU_GROUP_DIM]
            ss.append(jnp.dot(ws, vc, preferred_element_type=F32) + bias)
        y = (u * jnp.concatenate(ss, axis=0)).astype(BF16)
        acc = acc + jnp.dot(y, wout_ref[c0:c0 + SGU_GROUP_DIM, :], preferred_element_type=F32)
    x2 = x_ref[...] + acc
    xo_ref[...] = x2
    ho_ref[...] = _rmsnorm(x2, g_ref[...]).astype(BF16)


def _sgu_call(h, x, win, vg, ws, bs, wout, g, tm):
    T = x.shape[0]
    return pl.pallas_call(
        _sgu_kernel,
        grid=(T // tm,),
        in_specs=[
            pl.BlockSpec((tm, D_MODEL), lambda i: (i, 0)),
            pl.BlockSpec((tm, D_MODEL), lambda i: (i, 0)),
            _const_spec((D_MODEL, 2 * SGU_INNER)),
            _const_spec((1, SGU_INNER)),
            _const_spec((SGU_GROUPS, SGU_CHUNK, SGU_CHUNK)),
            _const_spec((SGU_CHUNK, SGU_INNER)),
            _const_spec((SGU_INNER, D_MODEL)),
            _const_spec((1, D_MODEL)),
        ],
        out_specs=[
            pl.BlockSpec((tm, D_MODEL), lambda i: (i, 0)),
            pl.BlockSpec((tm, D_MODEL), lambda i: (i, 0)),
        ],
        out_shape=[
            jax.ShapeDtypeStruct((T, D_MODEL), F32),
            jax.ShapeDtypeStruct((T, D_MODEL), BF16),
        ],
        compiler_params=_cparams("parallel"),
        name="sgu",
    )(h, x, win, vg, ws, bs, wout, g)


def _rope_tables(seq):
    rows = seq // GRID_W
    row_idx = jnp.broadcast_to(jnp.arange(rows)[:, None], (rows, GRID_W)).reshape(-1)
    col_idx = jnp.broadcast_to(jnp.arange(GRID_W)[None, :], (rows, GRID_W)).reshape(-1)
    inv_freq = ROPE_THETA ** (-jnp.arange(0, ROT_HALF, 2, dtype=F32) / ROT_HALF)
    ang_r = row_idx.astype(F32)[:, None] * inv_freq[None, :]
    ang_c = col_idx.astype(F32)[:, None] * inv_freq[None, :]
    ang = jnp.concatenate([ang_r, ang_r, ang_c, ang_c], axis=-1)
    cos = jnp.tile(jnp.cos(ang), (1, 2))
    sin = jnp.tile(jnp.sin(ang), (1, 2))
    second = (jnp.arange(LANES) % ROT_HALF) >= (ROT_HALF // 2)
    sa = jnp.where(second[None, :], sin, 0.0)
    sb = jnp.where(second[None, :], 0.0, -sin)
    return cos, sa, sb


def _trunk(x3, p, tq, tk):
    batch, seq, _ = x3.shape
    x = x3.reshape(batch * seq, D_MODEL)
    tm = 512
    cos, sa, sb = _rope_tables(seq)
    qm, k2, v2 = _qkv_call(x, p["g_mix0"], p["w_qkv"], p["gq"], p["gk"], cos, sa, sb, seq, tm)
    o = _attn_call(qm, k2, v2, batch, seq, tq, tk)
    x, h = _oproj_call(o, x, p["w_o"], p["g_ffn0"], tm)
    x, h = _ffn_call(h, x, p["w_up0"], p["cw0"], p["cb0"], p["w_dn0"], p["g_mix1"], seq, tm, True)
    x, h = _sgu_call(h, x, p["w_in"], p["vg"], p["w_s"], p["b_s"], p["w_out"], p["g_ffn1"], 256)
    x, _ = _ffn_call(h, x, p["w_up1"], p["cw1"], p["cb1"], p["w_dn1"], p["g_ffn1"], seq, tm, False)
    return x.reshape(batch, seq, D_MODEL)


def kernel(x_prompt, x_sample, norm_mix, norm_ffn, attn_w_qkv, attn_q_norm, attn_k_norm, attn_w_o,
           sgu_w_in, sgu_v_norm, sgu_w_s, sgu_b_s, sgu_w_out,
           ffn_w_up, ffn_conv_w, ffn_conv_b, ffn_w_down):
    scale = LOG2_E / math.sqrt(HEAD_DIM)
    p = {
        "g_mix0": norm_mix[0][None, :], "g_mix1": norm_mix[1][None, :],
        "g_ffn0": norm_ffn[0][None, :], "g_ffn1": norm_ffn[1][None, :],
        "w_qkv": attn_w_qkv[0].astype(BF16),
        "gq": jnp.tile(attn_q_norm[0] * scale, 2)[None, :],
        "gk": jnp.tile(attn_k_norm[0], 2)[None, :],
        "w_o": attn_w_o[0].astype(BF16),
        "w_in": sgu_w_in[0].astype(BF16),
        "vg": sgu_v_norm[0][None, :],
        "w_s": sgu_w_s[0].astype(BF16),
        "b_s": jnp.repeat(sgu_b_s[0].T, SGU_GROUP_DIM, axis=1),
        "w_out": sgu_w_out[0].astype(BF16),
        "w_up0": ffn_w_up[0].astype(BF16), "w_up1": ffn_w_up[1].astype(BF16),
        "cw0": ffn_conv_w[0], "cw1": ffn_conv_w[1],
        "cb0": ffn_conv_b[0][None, :], "cb1": ffn_conv_b[1][None, :],
        "w_dn0": ffn_w_down[0].astype(BF16), "w_dn1": ffn_w_down[1].astype(BF16),
    }
    y_prompt = _trunk(x_prompt, p, 512, 1024)
    y_sample = _trunk(x_sample, p, 512, 1024)
    return (y_prompt, y_sample)
```

```python
import functools
import math

import jax
import jax.numpy as jnp
from jax import lax
from jax.experimental import pallas as pl
from jax.experimental.pallas import tpu as pltpu

D_MODEL = 1024
N_HEADS = 16
HEAD_DIM = 64
N_KV_HEADS = 4
Q_PER_KV = N_HEADS // N_KV_HEADS
ROT_HALF = HEAD_DIM // 2
ROPE_THETA = 10000.0
GRID_W = 64
SGU_CHUNK = 128
SGU_INNER = 2 * D_MODEL
SGU_GROUPS = 8
SGU_GROUP_DIM = SGU_INNER // SGU_GROUPS
FFN_DIM = 2816
EPS = 1e-6
LOG2_E = math.log2(math.e)

LANES = 128
BF16_SUBLANES = 16
VMEM_LIMIT_BYTES = 56 * 1024 * 1024
N_PAIRS = N_HEADS // 2
N_KV_PAIRS = N_KV_HEADS // 2
QKV_OUT = (N_HEADS + 2 * N_KV_HEADS) * HEAD_DIM
K_OFF = N_HEADS * HEAD_DIM
V_OFF = K_OFF + N_KV_HEADS * HEAD_DIM
FFN_CHUNK = 256
N_FFN_CHUNKS = FFN_DIM // FFN_CHUNK

F32 = jnp.float32
BF16 = jnp.bfloat16


def _cparams(*sem):
    return pltpu.CompilerParams(dimension_semantics=sem, vmem_limit_bytes=VMEM_LIMIT_BYTES)


def _const_spec(shape):
    nd = len(shape)
    return pl.BlockSpec(shape, lambda *_: (0,) * nd, pipeline_mode=pl.Buffered(1))


def _rmsnorm(x, g):
    y = x * lax.rsqrt(jnp.mean(x * x, axis=-1, keepdims=True) + EPS)
    return y * g


def _qkv_kernel(x_ref, g_ref, w_ref, gq_ref, gk_ref, cos_ref, sa_ref, sb_ref,
                qm_ref, k2_ref, v2_ref):
    x = x_ref[...]
    h = _rmsnorm(x, g_ref[...]).astype(BF16)
    qkv = jnp.dot(h, w_ref[...], preferred_element_type=F32)
    tm = x.shape[0]
    lane = lax.broadcasted_iota(jnp.int32, (tm, LANES), 1)
    lo = lane < HEAD_DIM
    cos = cos_ref[...]
    sa = sa_ref[...]
    sb = sb_ref[...]

    row = lax.broadcasted_iota(jnp.int32, (2 * LANES, LANES), 0)
    col = lax.broadcasted_iota(jnp.int32, (2 * LANES, LANES), 1)
    avg = jnp.where((row % LANES) // HEAD_DIM == col // HEAD_DIM, 1.0 / HEAD_DIM, 0.0).astype(BF16)

    def norm_rope(blk, gain):
        sq = blk * blk
        hi = sq.astype(BF16)
        rem = (sq - hi.astype(F32)).astype(BF16)
        ms = jnp.dot(jnp.concatenate([hi, rem], axis=1), avg, preferred_element_type=F32)
        xn = blk * lax.rsqrt(ms + EPS) * gain
        return xn * cos + pltpu.roll(xn, 16, 1) * sa + pltpu.roll(xn, LANES - 16, 1) * sb

    gq = gq_ref[...]
    gk = gk_ref[...]
    for j in range(N_PAIRS):
        ro = norm_rope(qkv[:, LANES * j:LANES * (j + 1)], gq)
        qm_ref[2 * j] = jnp.where(lo, ro, 0.0).astype(BF16)
        qm_ref[2 * j + 1] = jnp.where(lo, 0.0, ro).astype(BF16)
    for j in range(N_KV_PAIRS):
        ro = norm_rope(qkv[:, K_OFF + LANES * j:K_OFF + LANES * (j + 1)], gk)
        sw = pltpu.roll(ro, HEAD_DIM, 1)
        k2_ref[2 * j] = jnp.where(lo, ro, sw).astype(BF16)
        k2_ref[2 * j + 1] = jnp.where(lo, sw, ro).astype(BF16)
        vv = qkv[:, V_OFF + LANES * j:V_OFF + LANES * (j + 1)]
        sw = pltpu.roll(vv, HEAD_DIM, 1)
        v2_ref[4 * j] = jnp.where(lo, vv, 1.0).astype(BF16)
        v2_ref[4 * j + 1] = jnp.where(lo, 1.0, sw).astype(BF16)
        v2_ref[4 * j + 2] = jnp.where(lo, sw, 1.0).astype(BF16)
        v2_ref[4 * j + 3] = jnp.where(lo, 1.0, vv).astype(BF16)


def _qkv_call(x, g, w, gq, gk, cos, sa, sb, seq, tm):
    T = x.shape[0]
    tiles_per_seq = seq // tm
    tab_spec = pl.BlockSpec((tm, LANES), lambda i: (i % tiles_per_seq, 0))
    return pl.pallas_call(
        _qkv_kernel,
        grid=(T // tm,),
        in_specs=[
            pl.BlockSpec((tm, D_MODEL), lambda i: (i, 0)),
            _const_spec((1, D_MODEL)),
            _const_spec((D_MODEL, QKV_OUT)),
            _const_spec((1, LANES)),
            _const_spec((1, LANES)),
            tab_spec, tab_spec, tab_spec,
        ],
        out_specs=[
            pl.BlockSpec((N_HEADS, tm, LANES), lambda i: (0, i, 0)),
            pl.BlockSpec((N_KV_HEADS, tm, LANES), lambda i: (0, i, 0)),
            pl.BlockSpec((2 * N_KV_HEADS, tm, LANES), lambda i: (0, i, 0)),
        ],
        out_shape=[
            jax.ShapeDtypeStruct((N_HEADS, T, LANES), BF16),
            jax.ShapeDtypeStruct((N_KV_HEADS, T, LANES), BF16),
            jax.ShapeDtypeStruct((2 * N_KV_HEADS, T, LANES), BF16),
        ],
        compiler_params=_cparams("parallel"),
        name="qkv",
    )(x, g, w, gq, gk, cos, sa, sb)


def _attn_kernel(nk, q_ref, k_ref, v_ref, o_ref, m_ref, acc_ref):
    j = pl.program_id(2)
    tq = q_ref.shape[1]
    tk = k_ref.shape[1]

    @pl.when(j == 0)
    def _():
        m_ref[...] = jnp.full(m_ref.shape, -jnp.inf, F32)
        acc_ref[...] = jnp.zeros(acc_ref.shape, F32)

    def scores(h):
        return lax.dot_general(q_ref[h], k_ref[h // Q_PER_KV], (((1,), (1,)), ((), ())),
                               preferred_element_type=F32)

    def softmax_pv(h, s):
        m_prev = m_ref[h]
        m_cur = s[:, :LANES]
        for c in range(1, tk // LANES):
            m_cur = jnp.maximum(m_cur, s[:, c * LANES:(c + 1) * LANES])
        m_next = jnp.maximum(m_prev, jnp.max(m_cur, axis=1, keepdims=True))
        alpha = jnp.exp2(m_prev - m_next)
        p = jnp.concatenate(
            [jnp.exp2(s[:, c * LANES:(c + 1) * LANES] - m_next).astype(BF16)
             for c in range(tk // LANES)], axis=1)
        v = v_ref[2 * (h // Q_PER_KV) + (h % 2)]
        m_ref[h] = m_next
        acc_ref[h] = alpha * acc_ref[h] + jnp.dot(p, v, preferred_element_type=F32)

    s_cur = scores(0)
    for h in range(N_HEADS):
        s_next = scores(h + 1) if h + 1 < N_HEADS else None
        softmax_pv(h, s_cur)
        s_cur = s_next

    @pl.when(j == nk - 1)
    def _():
        lane = lax.broadcasted_iota(jnp.int32, (tq, LANES), 1)
        lo = lane < HEAD_DIM
        for pr in range(N_PAIRS):
            a0 = acc_ref[2 * pr]
            a1 = acc_ref[2 * pr + 1]
            num = jnp.where(lo, a0, a1)
            den = pltpu.roll(jnp.where(lo, a1, a0), HEAD_DIM, 1)
            o_ref[pr] = (num / den).astype(BF16)


def _attn_call(qm, k2, v2, batch, seq, tq, tk):
    T = qm.shape[1]
    nq = seq // tq
    nk = seq // tk
    return pl.pallas_call(
        functools.partial(_attn_kernel, nk),
        grid=(batch, nq, nk),
        in_specs=[
            pl.BlockSpec((N_HEADS, tq, LANES), lambda b, i, j: (0, b * nq + i, 0)),
            pl.BlockSpec((N_KV_HEADS, tk, LANES), lambda b, i, j: (0, b * nk + j, 0)),
            pl.BlockSpec((2 * N_KV_HEADS, tk, LANES), lambda b, i, j: (0, b * nk + j, 0)),
        ],
        out_specs=pl.BlockSpec((N_PAIRS, tq, LANES), lambda b, i, j: (0, b * nq + i, 0)),
        out_shape=jax.ShapeDtypeStruct((N_PAIRS, T, LANES), BF16),
        scratch_shapes=[
            pltpu.VMEM((N_HEADS, tq, LANES), F32),
            pltpu.VMEM((N_HEADS, tq, LANES), F32),
        ],
        compiler_params=_cparams("parallel", "parallel", "arbitrary"),
        name="attn",
    )(qm, k2, v2)


def _oproj_kernel(o_ref, x_ref, w_ref, g_ref, xo_ref, h_ref):
    o = jnp.concatenate([o_ref[pr] for pr in range(N_PAIRS)], axis=1)
    x1 = x_ref[...] + jnp.dot(o, w_ref[...], preferred_element_type=F32)
    xo_ref[...] = x1
    h_ref[...] = _rmsnorm(x1, g_ref[...]).astype(BF16)


def _oproj_call(o, x, w, g, tm):
    T = x.shape[0]
    return pl.pallas_call(
        _oproj_kernel,
        grid=(T // tm,),
        in_specs=[
            pl.BlockSpec((N_PAIRS, tm, LANES), lambda i: (0, i, 0)),
            pl.BlockSpec((tm, D_MODEL), lambda i: (i, 0)),
            _const_spec((D_MODEL, D_MODEL)),
            _const_spec((1, D_MODEL)),
        ],
        out_specs=[
            pl.BlockSpec((tm, D_MODEL), lambda i: (i, 0)),
            pl.BlockSpec((tm, D_MODEL), lambda i: (i, 0)),
        ],
        out_shape=[
            jax.ShapeDtypeStruct((T, D_MODEL), F32),
            jax.ShapeDtypeStruct((T, D_MODEL), BF16),
        ],
        compiler_params=_cparams("parallel"),
        name="oproj",
    )(o, x, w, g)


def _ffn_kernel(tiles_per_seq, with_norm, hp_ref, h_ref, hn_ref, x_ref, wup_ref, cw_ref, cb_ref,
                wdn_ref, g_ref, xo_ref, *rest):
    i = pl.program_id(0)
    tm = h_ref.shape[0]
    halo = hp_ref.shape[0]
    pos = i % tiles_per_seq
    hp = hp_ref[...]
    hn = hn_ref[...]
    hp = jnp.where(pos == 0, jnp.zeros_like(hp), hp)
    hn = jnp.where(pos == tiles_per_seq - 1, jnp.zeros_like(hn), hn)
    hcat = jnp.concatenate([hp, h_ref[...], hn], axis=0)
    rows = tm + 2 * halo

    def up_proj(c0):
        return jnp.dot(hcat, wup_ref[:, c0:c0 + FFN_CHUNK], preferred_element_type=F32)

    def conv(y, c0):
        prev = pltpu.roll(y, 1, 0)[halo:halo + tm]
        nxt = pltpu.roll(y, rows - 1, 0)[halo:halo + tm]
        w = cw_ref[:, c0:c0 + FFN_CHUNK]
        return (prev * w[0:1] + y[halo:halo + tm] * w[1:2] + nxt * w[2:3]
                + cb_ref[:, c0:c0 + FFN_CHUNK])

    acts = []
    y_gate = up_proj(0)
    y_up = up_proj(FFN_DIM)
    for c in range(N_FFN_CHUNKS):
        if c + 1 < N_FFN_CHUNKS:
            y_gate_next = up_proj((c + 1) * FFN_CHUNK)
            y_up_next = up_proj(FFN_DIM + (c + 1) * FFN_CHUNK)
        gate = conv(y_gate, c * FFN_CHUNK)
        up = conv(y_up, FFN_DIM + c * FFN_CHUNK)
        y_gate, y_up = y_gate_next, y_up_next
        acts.append((gate * (1.0 / (1.0 + jnp.exp2(gate * (-LOG2_E)))) * up).astype(BF16))
    act = jnp.concatenate(acts, axis=1)
    x2 = x_ref[...] + jnp.dot(act, wdn_ref[...], preferred_element_type=F32)
    xo_ref[...] = x2
    if with_norm:
        rest[0][...] = _rmsnorm(x2, g_ref[...]).astype(BF16)


def _ffn_call(h, x, wup, cw, cb, wdn, g, seq, tm, with_norm):
    T = x.shape[0]
    halo = BF16_SUBLANES
    nblk = tm // halo
    last_blk = T // halo - 1
    out_specs = [pl.BlockSpec((tm, D_MODEL), lambda i: (i, 0))]
    out_shape = [jax.ShapeDtypeStruct((T, D_MODEL), F32)]
    if with_norm:
        out_specs.append(pl.BlockSpec((tm, D_MODEL), lambda i: (i, 0)))
        out_shape.append(jax.ShapeDtypeStruct((T, D_MODEL), BF16))
    res = pl.pallas_call(
        functools.partial(_ffn_kernel, seq // tm, with_norm),
        grid=(T // tm,),
        in_specs=[
            pl.BlockSpec((halo, D_MODEL), lambda i: (jnp.maximum(i * nblk - 1, 0), 0)),
            pl.BlockSpec((tm, D_MODEL), lambda i: (i, 0)),
            pl.BlockSpec((halo, D_MODEL), lambda i: (jnp.minimum((i + 1) * nblk, last_blk), 0)),
            pl.BlockSpec((tm, D_MODEL), lambda i: (i, 0)),
            _const_spec((D_MODEL, 2 * FFN_DIM)),
            _const_spec((3, 2 * FFN_DIM)),
            _const_spec((1, 2 * FFN_DIM)),
            _const_spec((FFN_DIM, D_MODEL)),
            _const_spec((1, D_MODEL)),
        ],
        out_specs=out_specs,
        out_shape=out_shape,
        compiler_params=_cparams("parallel"),
        name="ffn",
    )(h, h, h, x, wup, cw, cb, wdn, g)
    return res if with_norm else (res[0], None)


def _gelu(x):
    c = math.sqrt(2.0 / math.pi)
    return 0.5 * x * (1.0 + jnp.tanh(c * (x + 0.044715 * (x * x * x))))


def _sgu_kernel(h_ref, x_ref, win_ref, vg_ref, ws_ref, bs_ref, wout_ref, g_ref, xo_ref, ho_ref):
    tm = h_ref.shape[0]
    h = h_ref[...]
    v = _gelu(jnp.dot(h, win_ref[:, SGU_INNER:], preferred_element_type=F32))
    v = _rmsnorm(v, vg_ref[...]).astype(BF16)
    ys = []

    def u_proj(g):
        return jnp.dot(h, win_ref[:, g * SGU_GROUP_DIM:(g + 1) * SGU_GROUP_DIM],
                       preferred_element_type=F32)

    u_pre = u_proj(0)
    for g in range(SGU_GROUPS):
        c0 = g * SGU_GROUP_DIM
        u_pre_next = u_proj(g + 1) if g + 1 < SGU_GROUPS else None
        u = _gelu(u_pre)
        u_pre = u_pre_next
        ws = ws_ref[g]
        bias = bs_ref[:, c0:c0 + SGU_GROUP_DIM]
        ss = []
        for c in range(tm // SGU_CHUNK):
            vc = v[c * SGU_CHUNK:(c + 1) * SGU_CHUNK, c0:c0 + SGU_GROUP_DIM]
            ss.append(jnp.dot(ws, vc, preferred_element_type=F32) + bias)
        ys.append((u * jnp.concatenate(ss, axis=0)).astype(BF16))
    y = jnp.concatenate(ys, axis=1)
    x2 = x_ref[...] + jnp.dot(y, wout_ref[...], preferred_element_type=F32)
    xo_ref[...] = x2
    ho_ref[...] = _rmsnorm(x2, g_ref[...]).astype(BF16)


def _sgu_call(h, x, win, vg, ws, bs, wout, g, tm):
    T = x.shape[0]
    return pl.pallas_call(
        _sgu_kernel,
        grid=(T // tm,),
        in_specs=[
            pl.BlockSpec((tm, D_MODEL), lambda i: (i, 0)),
            pl.BlockSpec((tm, D_MODEL), lambda i: (i, 0)),
            _const_spec((D_MODEL, 2 * SGU_INNER)),
            _const_spec((1, SGU_INNER)),
            _const_spec((SGU_GROUPS, SGU_CHUNK, SGU_CHUNK)),
            _const_spec((SGU_CHUNK, SGU_INNER)),
            _const_spec((SGU_INNER, D_MODEL)),
            _const_spec((1, D_MODEL)),
        ],
        out_specs=[
            pl.BlockSpec((tm, D_MODEL), lambda i: (i, 0)),
            pl.BlockSpec((tm, D_MODEL), lambda i: (i, 0)),
        ],
        out_shape=[
            jax.ShapeDtypeStruct((T, D_MODEL), F32),
            jax.ShapeDtypeStruct((T, D_MODEL), BF16),
        ],
        compiler_params=_cparams("parallel"),
        name="sgu",
    )(h, x, win, vg, ws, bs, wout, g)


def _rope_tables(seq):
    rows = seq // GRID_W
    row_idx = jnp.broadcast_to(jnp.arange(rows)[:, None], (rows, GRID_W)).reshape(-1)
    col_idx = jnp.broadcast_to(jnp.arange(GRID_W)[None, :], (rows, GRID_W)).reshape(-1)
    inv_freq = ROPE_THETA ** (-jnp.arange(0, ROT_HALF, 2, dtype=F32) / ROT_HALF)
    ang_r = row_idx.astype(F32)[:, None] * inv_freq[None, :]
    ang_c = col_idx.astype(F32)[:, None] * inv_freq[None, :]
    ang = jnp.concatenate([ang_r, ang_r, ang_c, ang_c], axis=-1)
    cos = jnp.tile(jnp.cos(ang), (1, 2))
    sin = jnp.tile(jnp.sin(ang), (1, 2))
    second = (jnp.arange(LANES) % ROT_HALF) >= (ROT_HALF // 2)
    sa = jnp.where(second[None, :], sin, 0.0)
    sb = jnp.where(second[None, :], 0.0, -sin)
    return cos, sa, sb


def _trunk(x3, p, tq, tk):
    batch, seq, _ = x3.shape
    x = x3.reshape(batch * seq, D_MODEL)
    tm = 512
    cos, sa, sb = _rope_tables(seq)
    qm, k2, v2 = _qkv_call(x, p["g_mix0"], p["w_qkv"], p["gq"], p["gk"], cos, sa, sb, seq, tm)
    o = _attn_call(qm, k2, v2, batch, seq, tq, tk)
    x, h = _oproj_call(o, x, p["w_o"], p["g_ffn0"], tm)
    x, h = _ffn_call(h, x, p["w_up0"], p["cw0"], p["cb0"], p["w_dn0"], p["g_mix1"], seq, tm, True)
    x, h = _sgu_call(h, x, p["w_in"], p["vg"], p["w_s"], p["b_s"], p["w_out"], p["g_ffn1"], tm)
    x, _ = _ffn_call(h, x, p["w_up1"], p["cw1"], p["cb1"], p["w_dn1"], p["g_ffn1"], seq, tm, False)
    return x.reshape(batch, seq, D_MODEL)


def kernel(x_prompt, x_sample, norm_mix, norm_ffn, attn_w_qkv, attn_q_norm, attn_k_norm, attn_w_o,
           sgu_w_in, sgu_v_norm, sgu_w_s, sgu_b_s, sgu_w_out,
           ffn_w_up, ffn_conv_w, ffn_conv_b, ffn_w_down):
    scale = LOG2_E / math.sqrt(HEAD_DIM)
    p = {
        "g_mix0": norm_mix[0][None, :], "g_mix1": norm_mix[1][None, :],
        "g_ffn0": norm_ffn[0][None, :], "g_ffn1": norm_ffn[1][None, :],
        "w_qkv": attn_w_qkv[0].astype(BF16),
        "gq": jnp.tile(attn_q_norm[0] * scale, 2)[None, :],
        "gk": jnp.tile(attn_k_norm[0], 2)[None, :],
        "w_o": attn_w_o[0].astype(BF16),
        "w_in": sgu_w_in[0].astype(BF16),
        "vg": sgu_v_norm[0][None, :],
        "w_s": sgu_w_s[0].astype(BF16),
        "b_s": jnp.repeat(sgu_b_s[0].T, SGU_GROUP_DIM, axis=1),
        "w_out": sgu_w_out[0].astype(BF16),
        "w_up0": ffn_w_up[0].astype(BF16), "w_up1": ffn_w_up[1].astype(BF16),
        "cw0": ffn_conv_w[0], "cw1": ffn_conv_w[1],
        "cb0": ffn_conv_b[0][None, :], "cb1": ffn_conv_b[1][None, :],
        "w_dn0": ffn_w_down[0].astype(BF16), "w_dn1": ffn_w_down[1].astype(BF16),
    }
    y_prompt = _trunk(x_prompt, p, 512, 2048)
    y_sample = _trunk(x_sample, p, 512, 2048)
    return (y_prompt, y_sample)
```

```python
import functools
import math

import jax
import jax.numpy as jnp
import numpy as np
from jax import lax
from jax.experimental import pallas as pl
from jax.experimental.pallas import tpu as pltpu

D_MODEL = 1024
N_HEADS = 16
HEAD_DIM = 64
N_KV_HEADS = 4
Q_PER_KV = N_HEADS // N_KV_HEADS
ROT_HALF = HEAD_DIM // 2
ROPE_THETA = 10000.0
GRID_W = 64
SGU_CHUNK = 128
SGU_INNER = 2 * D_MODEL
SGU_GROUPS = 8
SGU_GROUP_DIM = SGU_INNER // SGU_GROUPS
FFN_DIM = 2816
EPS = 1e-6
LOG2_E = math.log2(math.e)

LANES = 128
BF16_SUBLANES = 16
VMEM_LIMIT_BYTES = 56 * 1024 * 1024
N_PAIRS = N_HEADS // 2
N_KV_PAIRS = N_KV_HEADS // 2
QKV_OUT = (N_HEADS + 2 * N_KV_HEADS) * HEAD_DIM
K_OFF = N_HEADS * HEAD_DIM
V_OFF = K_OFF + N_KV_HEADS * HEAD_DIM
FFN_CHUNK = 256
N_FFN_CHUNKS = FFN_DIM // FFN_CHUNK

F32 = jnp.float32
BF16 = jnp.bfloat16


def _cparams(*sem):
    return pltpu.CompilerParams(dimension_semantics=sem, vmem_limit_bytes=VMEM_LIMIT_BYTES)


def _const_spec(shape):
    nd = len(shape)
    return pl.BlockSpec(shape, lambda *_: (0,) * nd, pipeline_mode=pl.Buffered(1))


def _rmsnorm(x, g):
    y = x * lax.rsqrt(jnp.mean(x * x, axis=-1, keepdims=True) + EPS)
    return y * g


def _qkv_kernel(x_ref, g_ref, w_ref, gq_ref, gk_ref, cos_ref, sa_ref, sb_ref,
                qm_ref, k2_ref, v2_ref):
    x = x_ref[...]
    h = _rmsnorm(x, g_ref[...]).astype(BF16)
    qkv = jnp.dot(h, w_ref[...], preferred_element_type=F32)
    tm = x.shape[0]
    lane = lax.broadcasted_iota(jnp.int32, (tm, LANES), 1)
    lo = lane < HEAD_DIM
    cos = cos_ref[...]
    sa = sa_ref[...]
    sb = sb_ref[...]

    row = lax.broadcasted_iota(jnp.int32, (2 * LANES, LANES), 0)
    col = lax.broadcasted_iota(jnp.int32, (2 * LANES, LANES), 1)
    avg = jnp.where((row % LANES) // HEAD_DIM == col // HEAD_DIM, 1.0 / HEAD_DIM, 0.0).astype(BF16)

    def norm_rope(blk, gain):
        sq = blk * blk
        hi = sq.astype(BF16)
        rem = (sq - hi.astype(F32)).astype(BF16)
        ms = jnp.dot(jnp.concatenate([hi, rem], axis=1), avg, preferred_element_type=F32)
        xn = blk * lax.rsqrt(ms + EPS) * gain
        return xn * cos + pltpu.roll(xn, 16, 1) * sa + pltpu.roll(xn, LANES - 16, 1) * sb

    gq = gq_ref[...]
    gk = gk_ref[...]
    for j in range(N_PAIRS):
        ro = norm_rope(qkv[:, LANES * j:LANES * (j + 1)], gq)
        qm_ref[2 * j] = jnp.where(lo, ro, 0.0).astype(BF16)
        qm_ref[2 * j + 1] = jnp.where(lo, 0.0, ro).astype(BF16)
    for j in range(N_KV_PAIRS):
        ro = norm_rope(qkv[:, K_OFF + LANES * j:K_OFF + LANES * (j + 1)], gk)
        sw = pltpu.roll(ro, HEAD_DIM, 1)
        k2_ref[2 * j] = jnp.where(lo, ro, sw).astype(BF16)
        k2_ref[2 * j + 1] = jnp.where(lo, sw, ro).astype(BF16)
        vv = qkv[:, V_OFF + LANES * j:V_OFF + LANES * (j + 1)]
        sw = pltpu.roll(vv, HEAD_DIM, 1)
        v2_ref[4 * j] = jnp.where(lo, vv, 1.0).astype(BF16)
        v2_ref[4 * j + 1] = jnp.where(lo, 1.0, sw).astype(BF16)
        v2_ref[4 * j + 2] = jnp.where(lo, sw, 1.0).astype(BF16)
        v2_ref[4 * j + 3] = jnp.where(lo, 1.0, vv).astype(BF16)


def _qkv_call(x, g, w, gq, gk, cos, sa, sb, seq, tm):
    T = x.shape[0]
    tiles_per_seq = seq // tm
    tab_spec = pl.BlockSpec((tm, LANES), lambda i: (i % tiles_per_seq, 0))
    return pl.pallas_call(
        _qkv_kernel,
        grid=(T // tm,),
        in_specs=[
            pl.BlockSpec((tm, D_MODEL), lambda i: (i, 0)),
            _const_spec((1, D_MODEL)),
            _const_spec((D_MODEL, QKV_OUT)),
            _const_spec((1, LANES)),
            _const_spec((1, LANES)),
            tab_spec, tab_spec, tab_spec,
        ],
        out_specs=[
            pl.BlockSpec((N_HEADS, tm, LANES), lambda i: (0, i, 0)),
            pl.BlockSpec((N_KV_HEADS, tm, LANES), lambda i: (0, i, 0)),
            pl.BlockSpec((2 * N_KV_HEADS, tm, LANES), lambda i: (0, i, 0)),
        ],
        out_shape=[
            jax.ShapeDtypeStruct((N_HEADS, T, LANES), BF16),
            jax.ShapeDtypeStruct((N_KV_HEADS, T, LANES), BF16),
            jax.ShapeDtypeStruct((2 * N_KV_HEADS, T, LANES), BF16),
        ],
        compiler_params=_cparams("parallel"),
        name="qkv",
    )(x, g, w, gq, gk, cos, sa, sb)


def _attn_kernel(nk, q_ref, k_ref, v_ref, x_ref, wo_ref, g_ref, xo_ref, h_ref, m_ref, acc_ref):
    j = pl.program_id(2)
    tq = q_ref.shape[1]
    tk = k_ref.shape[1]

    @pl.when(j == 0)
    def _():
        m_ref[...] = jnp.full(m_ref.shape, -jnp.inf, F32)
        acc_ref[...] = jnp.zeros(acc_ref.shape, F32)

    def scores(h):
        return lax.dot_general(q_ref[h], k_ref[h // Q_PER_KV], (((1,), (1,)), ((), ())),
                               preferred_element_type=F32)

    def softmax_pv(h, s):
        m_prev = m_ref[h]
        m_cur = s[:, :LANES]
        for c in range(1, tk // LANES):
            m_cur = jnp.maximum(m_cur, s[:, c * LANES:(c + 1) * LANES])
        m_next = jnp.maximum(m_prev, jnp.max(m_cur, axis=1, keepdims=True))
        alpha = jnp.exp2(m_prev - m_next)
        p = jnp.concatenate(
            [jnp.exp2(s[:, c * LANES:(c + 1) * LANES] - m_next).astype(BF16)
             for c in range(tk // LANES)], axis=1)
        v = v_ref[2 * (h // Q_PER_KV) + (h % 2)]
        m_ref[h] = m_next
        acc_ref[h] = alpha * acc_ref[h] + jnp.dot(p, v, preferred_element_type=F32)

    s_cur = scores(0)
    for h in range(N_HEADS):
        s_next = scores(h + 1) if h + 1 < N_HEADS else None
        softmax_pv(h, s_cur)
        s_cur = s_next

    @pl.when(j == nk - 1)
    def _():
        lane = lax.broadcasted_iota(jnp.int32, (tq, LANES), 1)
        lo = lane < HEAD_DIM
        pairs = []
        for pr in range(N_PAIRS):
            a0 = acc_ref[2 * pr]
            a1 = acc_ref[2 * pr + 1]
            num = jnp.where(lo, a0, a1)
            den = pltpu.roll(jnp.where(lo, a1, a0), HEAD_DIM, 1)
            pairs.append((num / den).astype(BF16))
        o = jnp.concatenate(pairs, axis=1)
        x1 = x_ref[...] + jnp.dot(o, wo_ref[...], preferred_element_type=F32)
        xo_ref[...] = x1
        h_ref[...] = _rmsnorm(x1, g_ref[...]).astype(BF16)


def _attn_call(qm, k2, v2, x, wo, g, batch, seq, tq, tk):
    T = qm.shape[1]
    nq = seq // tq
    nk = seq // tk
    row_spec = pl.BlockSpec((tq, D_MODEL), lambda b, i, j: (b * nq + i, 0))
    return pl.pallas_call(
        functools.partial(_attn_kernel, nk),
        grid=(batch, nq, nk),
        in_specs=[
            pl.BlockSpec((N_HEADS, tq, LANES), lambda b, i, j: (0, b * nq + i, 0)),
            pl.BlockSpec((N_KV_HEADS, tk, LANES), lambda b, i, j: (0, b * nk + j, 0)),
            pl.BlockSpec((2 * N_KV_HEADS, tk, LANES), lambda b, i, j: (0, b * nk + j, 0)),
            row_spec,
            _const_spec((D_MODEL, D_MODEL)),
            _const_spec((1, D_MODEL)),
        ],
        out_specs=[row_spec, row_spec],
        out_shape=[
            jax.ShapeDtypeStruct((T, D_MODEL), F32),
            jax.ShapeDtypeStruct((T, D_MODEL), BF16),
        ],
        scratch_shapes=[
            pltpu.VMEM((N_HEADS, tq, LANES), F32),
            pltpu.VMEM((N_HEADS, tq, LANES), F32),
        ],
        compiler_params=_cparams("parallel", "parallel", "arbitrary"),
        name="attn",
    )(qm, k2, v2, x, wo, g)


def _ffn_kernel(tiles_per_seq, with_norm, hp_ref, h_ref, hn_ref, x_ref, wup_ref, cw_ref, cb_ref,
                wdn_ref, g_ref, xo_ref, *rest):
    i = pl.program_id(0)
    tm = h_ref.shape[0]
    halo = hp_ref.shape[0]
    pos = i % tiles_per_seq
    hp = hp_ref[...]
    hn = hn_ref[...]
    hp = jnp.where(pos == 0, jnp.zeros_like(hp), hp)
    hn = jnp.where(pos == tiles_per_seq - 1, jnp.zeros_like(hn), hn)
    hcat = jnp.concatenate([hp, h_ref[...], hn], axis=0)
    rows = tm + 2 * halo

    def up_proj(c0):
        return jnp.dot(hcat, wup_ref[:, c0:c0 + FFN_CHUNK], preferred_element_type=F32)

    def conv(y, c0):
        prev = pltpu.roll(y, 1, 0)[halo:halo + tm]
        nxt = pltpu.roll(y, rows - 1, 0)[halo:halo + tm]
        w = cw_ref[:, c0:c0 + FFN_CHUNK]
        return (prev * w[0:1] + y[halo:halo + tm] * w[1:2] + nxt * w[2:3]
                + cb_ref[:, c0:c0 + FFN_CHUNK])

    acts = []
    y_gate = up_proj(0)
    y_up = up_proj(FFN_DIM)
    for c in range(N_FFN_CHUNKS):
        if c + 1 < N_FFN_CHUNKS:
            y_gate_next = up_proj((c + 1) * FFN_CHUNK)
            y_up_next = up_proj(FFN_DIM + (c + 1) * FFN_CHUNK)
        gate = conv(y_gate, c * FFN_CHUNK)
        up = conv(y_up, FFN_DIM + c * FFN_CHUNK)
        y_gate, y_up = y_gate_next, y_up_next
        acts.append((gate * (1.0 / (1.0 + jnp.exp2(gate * (-LOG2_E)))) * up).astype(BF16))
    act = jnp.concatenate(acts, axis=1)
    x2 = x_ref[...] + jnp.dot(act, wdn_ref[...], preferred_element_type=F32)
    xo_ref[...] = x2
    if with_norm:
        rest[0][...] = _rmsnorm(x2, g_ref[...]).astype(BF16)


def _ffn_call(h, x, wup, cw, cb, wdn, g, seq, tm, with_norm):
    T = x.shape[0]
    halo = BF16_SUBLANES
    nblk = tm // halo
    last_blk = T // halo - 1
    out_specs = [pl.BlockSpec((tm, D_MODEL), lambda i: (i, 0))]
    out_shape = [jax.ShapeDtypeStruct((T, D_MODEL), F32)]
    if with_norm:
        out_specs.append(pl.BlockSpec((tm, D_MODEL), lambda i: (i, 0)))
        out_shape.append(jax.ShapeDtypeStruct((T, D_MODEL), BF16))
    res = pl.pallas_call(
        functools.partial(_ffn_kernel, seq // tm, with_norm),
        grid=(T // tm,),
        in_specs=[
            pl.BlockSpec((halo, D_MODEL), lambda i: (jnp.maximum(i * nblk - 1, 0), 0)),
            pl.BlockSpec((tm, D_MODEL), lambda i: (i, 0)),
            pl.BlockSpec((halo, D_MODEL), lambda i: (jnp.minimum((i + 1) * nblk, last_blk), 0)),
            pl.BlockSpec((tm, D_MODEL), lambda i: (i, 0)),
            _const_spec((D_MODEL, 2 * FFN_DIM)),
            _const_spec((3, 2 * FFN_DIM)),
            _const_spec((1, 2 * FFN_DIM)),
            _const_spec((FFN_DIM, D_MODEL)),
            _const_spec((1, D_MODEL)),
        ],
        out_specs=out_specs,
        out_shape=out_shape,
        compiler_params=_cparams("parallel"),
        name="ffn",
    )(h, h, h, x, wup, cw, cb, wdn, g)
    return res if with_norm else (res[0], None)


def _gelu(x):
    c = math.sqrt(2.0 / math.pi)
    return 0.5 * x * (1.0 + jnp.tanh(c * (x + 0.044715 * (x * x * x))))


def _sgu_kernel(h_ref, x_ref, win_ref, vg_ref, ws_ref, bs_ref, wout_ref, g_ref, xo_ref, ho_ref):
    tm = h_ref.shape[0]
    h = h_ref[...]
    v = _gelu(jnp.dot(h, win_ref[:, SGU_INNER:], preferred_element_type=F32))
    v = _rmsnorm(v, vg_ref[...]).astype(BF16)
    ys = []

    def u_proj(g):
        return jnp.dot(h, win_ref[:, g * SGU_GROUP_DIM:(g + 1) * SGU_GROUP_DIM],
                       preferred_element_type=F32)

    u_pre = u_proj(0)
    for g in range(SGU_GROUPS):
        c0 = g * SGU_GROUP_DIM
        u_pre_next = u_proj(g + 1) if g + 1 < SGU_GROUPS else None
        u = _gelu(u_pre)
        u_pre = u_pre_next
        ws = ws_ref[g]
        bias = bs_ref[:, c0:c0 + SGU_GROUP_DIM]
        ss = []
        for c in range(tm // SGU_CHUNK):
            vc = v[c * SGU_CHUNK:(c + 1) * SGU_CHUNK, c0:c0 + SGU_GROUP_DIM]
            ss.append(jnp.dot(ws, vc, preferred_element_type=F32) + bias)
        ys.append((u * jnp.concatenate(ss, axis=0)).astype(BF16))
    y = jnp.concatenate(ys, axis=1)
    x2 = x_ref[...] + jnp.dot(y, wout_ref[...], preferred_element_type=F32)
    xo_ref[...] = x2
    ho_ref[...] = _rmsnorm(x2, g_ref[...]).astype(BF16)


def _sgu_call(h, x, win, vg, ws, bs, wout, g, tm):
    T = x.shape[0]
    return pl.pallas_call(
        _sgu_kernel,
        grid=(T // tm,),
        in_specs=[
            pl.BlockSpec((tm, D_MODEL), lambda i: (i, 0)),
            pl.BlockSpec((tm, D_MODEL), lambda i: (i, 0)),
            _const_spec((D_MODEL, 2 * SGU_INNER)),
            _const_spec((1, SGU_INNER)),
            _const_spec((SGU_GROUPS, SGU_CHUNK, SGU_CHUNK)),
            _const_spec((SGU_CHUNK, SGU_INNER)),
            _const_spec((SGU_INNER, D_MODEL)),
            _const_spec((1, D_MODEL)),
        ],
        out_specs=[
            pl.BlockSpec((tm, D_MODEL), lambda i: (i, 0)),
            pl.BlockSpec((tm, D_MODEL), lambda i: (i, 0)),
        ],
        out_shape=[
            jax.ShapeDtypeStruct((T, D_MODEL), F32),
            jax.ShapeDtypeStruct((T, D_MODEL), BF16),
        ],
        compiler_params=_cparams("parallel"),
        name="sgu",
    )(h, x, win, vg, ws, bs, wout, g)


def _rope_tables(seq):
    rows = seq // GRID_W
    row_idx = np.repeat(np.arange(rows), GRID_W).astype(np.float32)
    col_idx = np.tile(np.arange(GRID_W), rows).astype(np.float32)
    inv_freq = (np.float32(ROPE_THETA)
                ** (-np.arange(0, ROT_HALF, 2, dtype=np.float32) / np.float32(ROT_HALF)))
    ang_r = row_idx[:, None] * inv_freq[None, :]
    ang_c = col_idx[:, None] * inv_freq[None, :]
    ang = np.concatenate([ang_r, ang_r, ang_c, ang_c], axis=-1)
    cos = np.tile(np.cos(ang), (1, 2))
    sin = np.tile(np.sin(ang), (1, 2))
    second = (np.arange(LANES) % ROT_HALF) >= (ROT_HALF // 2)
    sa = np.where(second[None, :], sin, 0.0).astype(np.float32)
    sb = np.where(second[None, :], 0.0, -sin).astype(np.float32)
    return jnp.asarray(cos, F32), jnp.asarray(sa, F32), jnp.asarray(sb, F32)


def _trunk(x3, p, tq, tk):
    batch, seq, _ = x3.shape
    x = x3.reshape(batch * seq, D_MODEL)
    tm = 512
    cos, sa, sb = _rope_tables(seq)
    qm, k2, v2 = _qkv_call(x, p["g_mix0"], p["w_qkv"], p["gq"], p["gk"], cos, sa, sb, seq, tm)
    x, h = _attn_call(qm, k2, v2, x, p["w_o"], p["g_ffn0"], batch, seq, tq, tk)
    x, h = _ffn_call(h, x, p["w_up0"], p["cw0"], p["cb0"], p["w_dn0"], p["g_mix1"], seq, tm, True)
    x, h = _sgu_call(h, x, p["w_in"], p["vg"], p["w_s"], p["b_s"], p["w_out"], p["g_ffn1"], tm)
    x, _ = _ffn_call(h, x, p["w_up1"], p["cw1"], p["cb1"], p["w_dn1"], p["g_ffn1"], seq, tm, False)
    return x.reshape(batch, seq, D_MODEL)


def kernel(x_prompt, x_sample, norm_mix, norm_ffn, attn_w_qkv, attn_q_norm, attn_k_norm, attn_w_o,
           sgu_w_in, sgu_v_norm, sgu_w_s, sgu_b_s, sgu_w_out,
           ffn_w_up, ffn_conv_w, ffn_conv_b, ffn_w_down):
    scale = LOG2_E / math.sqrt(HEAD_DIM)
    p = {
        "g_mix0": norm_mix[0][None, :], "g_mix1": norm_mix[1][None, :],
        "g_ffn0": norm_ffn[0][None, :], "g_ffn1": norm_ffn[1][None, :],
        "w_qkv": attn_w_qkv[0].astype(BF16),
        "gq": jnp.tile(attn_q_norm[0] * scale, 2)[None, :],
        "gk": jnp.tile(attn_k_norm[0], 2)[None, :],
        "w_o": attn_w_o[0].astype(BF16),
        "w_in": sgu_w_in[0].astype(BF16),
        "vg": sgu_v_norm[0][None, :],
        "w_s": sgu_w_s[0].astype(BF16),
        "b_s": jnp.repeat(sgu_b_s[0].T, SGU_GROUP_DIM, axis=1),
        "w_out": sgu_w_out[0].astype(BF16),
        "w_up0": ffn_w_up[0].astype(BF16), "w_up1": ffn_w_up[1].astype(BF16),
        "cw0": ffn_conv_w[0], "cw1": ffn_conv_w[1],
        "cb0": ffn_conv_b[0][None, :], "cb1": ffn_conv_b[1][None, :],
        "w_dn0": ffn_w_down[0].astype(BF16), "w_dn1": ffn_w_down[1].astype(BF16),
    }
    y_prompt = _trunk(x_prompt, p, 512, 2048)
    y_sample = _trunk(x_sample, p, 512, 2048)
    return (y_prompt, y_sample)
```

```python
import functools
import math

import jax
import jax.numpy as jnp
import numpy as np
from jax import lax
from jax.experimental import pallas as pl
from jax.experimental.pallas import tpu as pltpu

D_MODEL = 1024
N_HEADS = 16
HEAD_DIM = 64
N_KV_HEADS = 4
Q_PER_KV = N_HEADS // N_KV_HEADS
ROT_HALF = HEAD_DIM // 2
ROPE_THETA = 10000.0
GRID_W = 64
SGU_CHUNK = 128
SGU_INNER = 2 * D_MODEL
SGU_GROUPS = 8
SGU_GROUP_DIM = SGU_INNER // SGU_GROUPS
FFN_DIM = 2816
EPS = 1e-6
LOG2_E = math.log2(math.e)

LANES = 128
BF16_SUBLANES = 16
VMEM_LIMIT_BYTES = 56 * 1024 * 1024
N_PAIRS = N_HEADS // 2
N_KV_PAIRS = N_KV_HEADS // 2
QKV_OUT = (N_HEADS + 2 * N_KV_HEADS) * HEAD_DIM
K_OFF = N_HEADS * HEAD_DIM
V_OFF = K_OFF + N_KV_HEADS * HEAD_DIM
PROJ_TILE = 2 * LANES
FFN_CHUNK = 256
N_FFN_CHUNKS = FFN_DIM // FFN_CHUNK

F32 = jnp.float32
BF16 = jnp.bfloat16


def _cparams(*sem):
    return pltpu.CompilerParams(dimension_semantics=sem, vmem_limit_bytes=VMEM_LIMIT_BYTES)


def _const_spec(shape):
    nd = len(shape)
    return pl.BlockSpec(shape, lambda *_: (0,) * nd, pipeline_mode=pl.Buffered(1))


def _rmsnorm(x, g):
    y = x * lax.rsqrt(jnp.mean(x * x, axis=-1, keepdims=True) + EPS)
    return y * g


def _qkv_kernel(x_ref, g_ref, w_ref, gq_ref, gk_ref, cos_ref, sa_ref, sb_ref,
                qm_ref, k2_ref, v2_ref):
    x = x_ref[...]
    h = _rmsnorm(x, g_ref[...]).astype(BF16)
    tm = x.shape[0]
    lane = lax.broadcasted_iota(jnp.int32, (tm, LANES), 1)
    lo = lane < HEAD_DIM
    cos = cos_ref[...]
    sa = sa_ref[...]
    sb = sb_ref[...]

    row = lax.broadcasted_iota(jnp.int32, (2 * LANES, LANES), 0)
    col = lax.broadcasted_iota(jnp.int32, (2 * LANES, LANES), 1)
    avg = jnp.where((row % LANES) // HEAD_DIM == col // HEAD_DIM, 1.0 / HEAD_DIM, 0.0).astype(BF16)

    def norm_rope(blk, gain):
        sq = blk * blk
        hi = sq.astype(BF16)
        rem = (sq - hi.astype(F32)).astype(BF16)
        ms = jnp.dot(jnp.concatenate([hi, rem], axis=1), avg, preferred_element_type=F32)
        xn = blk * lax.rsqrt(ms + EPS) * gain
        return xn * cos + pltpu.roll(xn, 16, 1) * sa + pltpu.roll(xn, LANES - 16, 1) * sb

    gq = gq_ref[...]
    gk = gk_ref[...]
    n_tiles = QKV_OUT // PROJ_TILE

    def zero_after(v):
        return jnp.minimum(jnp.maximum(v, 0.0), 0.0).astype(BF16)

    def proj(t, dep):
        lhs = h if dep is None else jnp.concatenate([h[:, :LANES] + dep, h[:, LANES:]], axis=1)
        return jnp.dot(lhs, w_ref[:, t * PROJ_TILE:(t + 1) * PROJ_TILE], preferred_element_type=F32)

    def post(t, y):
        last = None
        for half in range(PROJ_TILE // LANES):
            j = t * (PROJ_TILE // LANES) + half
            blk = y[:, LANES * half:LANES * (half + 1)]
            if j < N_PAIRS:
                ro = norm_rope(blk, gq)
                qm_ref[2 * j] = jnp.where(lo, ro, 0.0).astype(BF16)
                qm_ref[2 * j + 1] = jnp.where(lo, 0.0, ro).astype(BF16)
                last = ro
            elif j < N_PAIRS + N_KV_PAIRS:
                jk = j - N_PAIRS
                ro = norm_rope(blk, gk)
                sw = pltpu.roll(ro, HEAD_DIM, 1)
                k2_ref[2 * jk] = jnp.where(lo, ro, sw).astype(BF16)
                k2_ref[2 * jk + 1] = jnp.where(lo, sw, ro).astype(BF16)
                last = ro
            else:
                jv = j - N_PAIRS - N_KV_PAIRS
                sw = pltpu.roll(blk, HEAD_DIM, 1)
                v2_ref[4 * jv] = jnp.where(lo, blk, 1.0).astype(BF16)
                v2_ref[4 * jv + 1] = jnp.where(lo, 1.0, sw).astype(BF16)
                v2_ref[4 * jv + 2] = jnp.where(lo, sw, 1.0).astype(BF16)
                v2_ref[4 * jv + 3] = jnp.where(lo, 1.0, blk).astype(BF16)
                last = sw
        return last

    ys = {0: proj(0, None), 1: proj(1, None)}
    for t in range(n_tiles):
        last = post(t, ys.pop(t))
        if t + 2 < n_tiles:
            ys[t + 2] = proj(t + 2, zero_after(last))


def _qkv_call(x, g, w, gq, gk, cos, sa, sb, seq, tm):
    T = x.shape[0]
    tiles_per_seq = seq // tm
    tab_spec = pl.BlockSpec((tm, LANES), lambda i: (i % tiles_per_seq, 0))
    return pl.pallas_call(
        _qkv_kernel,
        grid=(T // tm,),
        in_specs=[
            pl.BlockSpec((tm, D_MODEL), lambda i: (i, 0)),
            _const_spec((1, D_MODEL)),
            _const_spec((D_MODEL, QKV_OUT)),
            _const_spec((1, LANES)),
            _const_spec((1, LANES)),
            tab_spec, tab_spec, tab_spec,
        ],
        out_specs=[
            pl.BlockSpec((N_HEADS, tm, LANES), lambda i: (0, i, 0)),
            pl.BlockSpec((N_KV_HEADS, tm, LANES), lambda i: (0, i, 0)),
            pl.BlockSpec((2 * N_KV_HEADS, tm, LANES), lambda i: (0, i, 0)),
        ],
        out_shape=[
            jax.ShapeDtypeStruct((N_HEADS, T, LANES), BF16),
            jax.ShapeDtypeStruct((N_KV_HEADS, T, LANES), BF16),
            jax.ShapeDtypeStruct((2 * N_KV_HEADS, T, LANES), BF16),
        ],
        compiler_params=_cparams("parallel"),
        name="qkv",
    )(x, g, w, gq, gk, cos, sa, sb)


def _attn_kernel(nk, q_ref, k_ref, v_ref, x_ref, wo_ref, g_ref, xo_ref, h_ref, *scratch):
    j = pl.program_id(2)
    tq = q_ref.shape[1]
    tk = k_ref.shape[1]
    single = nk == 1
    if not single:
        m_ref, acc_ref = scratch

        @pl.when(j == 0)
        def _():
            m_ref[...] = jnp.full(m_ref.shape, -jnp.inf, F32)
            acc_ref[...] = jnp.zeros(acc_ref.shape, F32)

    def scores(h):
        return lax.dot_general(q_ref[h], k_ref[h // Q_PER_KV], (((1,), (1,)), ((), ())),
                               preferred_element_type=F32)

    def softmax_pv(h, s):
        m_cur = s[:, :LANES]
        for c in range(1, tk // LANES):
            m_cur = jnp.maximum(m_cur, s[:, c * LANES:(c + 1) * LANES])
        row_max = jnp.max(m_cur, axis=1, keepdims=True)
        if single:
            m_next = jnp.broadcast_to(row_max, (tq, LANES))
        else:
            m_prev = m_ref[h]
            m_next = jnp.maximum(m_prev, row_max)
        p = jnp.concatenate(
            [jnp.exp2(s[:, c * LANES:(c + 1) * LANES] - m_next).astype(BF16)
             for c in range(tk // LANES)], axis=1)
        v = v_ref[2 * (h // Q_PER_KV) + (h % 2)]
        pv = jnp.dot(p, v, preferred_element_type=F32)
        if single:
            return pv
        alpha = jnp.exp2(m_prev - m_next)
        m_ref[h] = m_next
        acc_ref[h] = alpha * acc_ref[h] + pv
        return None

    accs = []
    s_cur = scores(0)
    for h in range(N_HEADS):
        s_next = scores(h + 1) if h + 1 < N_HEADS else None
        accs.append(softmax_pv(h, s_cur))
        s_cur = s_next

    def epilogue():
        lane = lax.broadcasted_iota(jnp.int32, (tq, LANES), 1)
        lo = lane < HEAD_DIM
        pairs = []
        for pr in range(N_PAIRS):
            a0 = accs[2 * pr] if single else acc_ref[2 * pr]
            a1 = accs[2 * pr + 1] if single else acc_ref[2 * pr + 1]
            num = jnp.where(lo, a0, a1)
            den = pltpu.roll(jnp.where(lo, a1, a0), HEAD_DIM, 1)
            pairs.append((num / den).astype(BF16))
        o = jnp.concatenate(pairs, axis=1)
        x1 = x_ref[...] + jnp.dot(o, wo_ref[...], preferred_element_type=F32)
        xo_ref[...] = x1
        h_ref[...] = _rmsnorm(x1, g_ref[...]).astype(BF16)

    if single:
        epilogue()
    else:
        pl.when(j == nk - 1)(epilogue)


def _attn_call(qm, k2, v2, x, wo, g, batch, seq, tq, tk):
    T = qm.shape[1]
    nq = seq // tq
    nk = seq // tk
    row_spec = pl.BlockSpec((tq, D_MODEL), lambda b, i, j: (b * nq + i, 0))
    return pl.pallas_call(
        functools.partial(_attn_kernel, nk),
        grid=(batch, nq, nk),
        in_specs=[
            pl.BlockSpec((N_HEADS, tq, LANES), lambda b, i, j: (0, b * nq + i, 0)),
            pl.BlockSpec((N_KV_HEADS, tk, LANES), lambda b, i, j: (0, b * nk + j, 0)),
            pl.BlockSpec((2 * N_KV_HEADS, tk, LANES), lambda b, i, j: (0, b * nk + j, 0)),
            row_spec,
            _const_spec((D_MODEL, D_MODEL)),
            _const_spec((1, D_MODEL)),
        ],
        out_specs=[row_spec, row_spec],
        out_shape=[
            jax.ShapeDtypeStruct((T, D_MODEL), F32),
            jax.ShapeDtypeStruct((T, D_MODEL), BF16),
        ],
        scratch_shapes=[] if nk == 1 else [
            pltpu.VMEM((N_HEADS, tq, LANES), F32),
            pltpu.VMEM((N_HEADS, tq, LANES), F32),
        ],
        compiler_params=_cparams("parallel", "parallel", "arbitrary"),
        name="attn",
    )(qm, k2, v2, x, wo, g)


def _ffn_kernel(tiles_per_seq, with_norm, hp_ref, h_ref, hn_ref, x_ref, wup_ref, cw_ref, cb_ref,
                wdn_ref, g_ref, xo_ref, *rest):
    i = pl.program_id(0)
    tm = h_ref.shape[0]
    halo = hp_ref.shape[0]
    pos = i % tiles_per_seq
    hp = hp_ref[...]
    hn = hn_ref[...]
    hp = jnp.where(pos == 0, jnp.zeros_like(hp), hp)
    hn = jnp.where(pos == tiles_per_seq - 1, jnp.zeros_like(hn), hn)
    hcat = jnp.concatenate([hp, h_ref[...], hn], axis=0)
    rows = tm + 2 * halo

    def up_proj(c0):
        return jnp.dot(hcat, wup_ref[:, c0:c0 + FFN_CHUNK], preferred_element_type=F32)

    def conv(y, c0):
        prev = pltpu.roll(y, 1, 0)[halo:halo + tm]
        nxt = pltpu.roll(y, rows - 1, 0)[halo:halo + tm]
        w = cw_ref[:, c0:c0 + FFN_CHUNK]
        return (prev * w[0:1] + y[halo:halo + tm] * w[1:2] + nxt * w[2:3]
                + cb_ref[:, c0:c0 + FFN_CHUNK])

    acts = []
    y_gate = up_proj(0)
    y_up = up_proj(FFN_DIM)
    for c in range(N_FFN_CHUNKS):
        if c + 1 < N_FFN_CHUNKS:
            y_gate_next = up_proj((c + 1) * FFN_CHUNK)
            y_up_next = up_proj(FFN_DIM + (c + 1) * FFN_CHUNK)
        gate = conv(y_gate, c * FFN_CHUNK)
        up = conv(y_up, FFN_DIM + c * FFN_CHUNK)
        y_gate, y_up = y_gate_next, y_up_next
        acts.append((gate * (1.0 / (1.0 + jnp.exp2(gate * (-LOG2_E)))) * up).astype(BF16))
    act = jnp.concatenate(acts, axis=1)
    x2 = x_ref[...] + jnp.dot(act, wdn_ref[...], preferred_element_type=F32)
    xo_ref[...] = x2
    if with_norm:
        rest[0][...] = _rmsnorm(x2, g_ref[...]).astype(BF16)


def _ffn_call(h, x, wup, cw, cb, wdn, g, seq, tm, with_norm):
    T = x.shape[0]
    halo = BF16_SUBLANES
    nblk = tm // halo
    last_blk = T // halo - 1
    out_specs = [pl.BlockSpec((tm, D_MODEL), lambda i: (i, 0))]
    out_shape = [jax.ShapeDtypeStruct((T, D_MODEL), F32)]
    if with_norm:
        out_specs.append(pl.BlockSpec((tm, D_MODEL), lambda i: (i, 0)))
        out_shape.append(jax.ShapeDtypeStruct((T, D_MODEL), BF16))
    res = pl.pallas_call(
        functools.partial(_ffn_kernel, seq // tm, with_norm),
        grid=(T // tm,),
        in_specs=[
            pl.BlockSpec((halo, D_MODEL), lambda i: (jnp.maximum(i * nblk - 1, 0), 0)),
            pl.BlockSpec((tm, D_MODEL), lambda i: (i, 0)),
            pl.BlockSpec((halo, D_MODEL), lambda i: (jnp.minimum((i + 1) * nblk, last_blk), 0)),
            pl.BlockSpec((tm, D_MODEL), lambda i: (i, 0)),
            _const_spec((D_MODEL, 2 * FFN_DIM)),
            _const_spec((3, 2 * FFN_DIM)),
            _const_spec((1, 2 * FFN_DIM)),
            _const_spec((FFN_DIM, D_MODEL)),
            _const_spec((1, D_MODEL)),
        ],
        out_specs=out_specs,
        out_shape=out_shape,
        compiler_params=_cparams("parallel"),
        name="ffn",
    )(h, h, h, x, wup, cw, cb, wdn, g)
    return res if with_norm else (res[0], None)


def _gelu(x):
    c = math.sqrt(2.0 / math.pi)
    return 0.5 * x * (1.0 + jnp.tanh(c * (x + 0.044715 * (x * x * x))))


def _sgu_kernel(h_ref, x_ref, win_ref, vg_ref, ws_ref, bs_ref, wout_ref, g_ref, xo_ref, ho_ref):
    tm = h_ref.shape[0]
    h = h_ref[...]
    v = _gelu(jnp.dot(h, win_ref[:, SGU_INNER:], preferred_element_type=F32))
    v = _rmsnorm(v, vg_ref[...]).astype(BF16)
    ys = []

    def u_proj(g):
        return jnp.dot(h, win_ref[:, g * SGU_GROUP_DIM:(g + 1) * SGU_GROUP_DIM],
                       preferred_element_type=F32)

    u_pre = u_proj(0)
    for g in range(SGU_GROUPS):
        c0 = g * SGU_GROUP_DIM
        u_pre_next = u_proj(g + 1) if g + 1 < SGU_GROUPS else None
        u = _gelu(u_pre)
        u_pre = u_pre_next
        ws = ws_ref[g]
        bias = bs_ref[:, c0:c0 + SGU_GROUP_DIM]
        ss = []
        for c in range(tm // SGU_CHUNK):
            vc = v[c * SGU_CHUNK:(c + 1) * SGU_CHUNK, c0:c0 + SGU_GROUP_DIM]
            ss.append(jnp.dot(ws, vc, preferred_element_type=F32) + bias)
        ys.append((u * jnp.concatenate(ss, axis=0)).astype(BF16))
    y = jnp.concatenate(ys, axis=1)
    x2 = x_ref[...] + jnp.dot(y, wout_ref[...], preferred_element_type=F32)
    xo_ref[...] = x2
    ho_ref[...] = _rmsnorm(x2, g_ref[...]).astype(BF16)


def _sgu_call(h, x, win, vg, ws, bs, wout, g, tm):
    T = x.shape[0]
    return pl.pallas_call(
        _sgu_kernel,
        grid=(T // tm,),
        in_specs=[
            pl.BlockSpec((tm, D_MODEL), lambda i: (i, 0)),
            pl.BlockSpec((tm, D_MODEL), lambda i: (i, 0)),
            _const_spec((D_MODEL, 2 * SGU_INNER)),
            _const_spec((1, SGU_INNER)),
            _const_spec((SGU_GROUPS, SGU_CHUNK, SGU_CHUNK)),
            _const_spec((SGU_CHUNK, SGU_INNER)),
            _const_spec((SGU_INNER, D_MODEL)),
            _const_spec((1, D_MODEL)),
        ],
        out_specs=[
            pl.BlockSpec((tm, D_MODEL), lambda i: (i, 0)),
            pl.BlockSpec((tm, D_MODEL), lambda i: (i, 0)),
        ],
        out_shape=[
            jax.ShapeDtypeStruct((T, D_MODEL), F32),
            jax.ShapeDtypeStruct((T, D_MODEL), BF16),
        ],
        compiler_params=_cparams("parallel"),
        name="sgu",
    )(h, x, win, vg, ws, bs, wout, g)


def _rope_tables(seq):
    rows = seq // GRID_W
    row_idx = np.repeat(np.arange(rows), GRID_W).astype(np.float32)
    col_idx = np.tile(np.arange(GRID_W), rows).astype(np.float32)
    inv_freq = (np.float32(ROPE_THETA)
                ** (-np.arange(0, ROT_HALF, 2, dtype=np.float32) / np.float32(ROT_HALF)))
    ang_r = row_idx[:, None] * inv_freq[None, :]
    ang_c = col_idx[:, None] * inv_freq[None, :]
    ang = np.concatenate([ang_r, ang_r, ang_c, ang_c], axis=-1)
    cos = np.tile(np.cos(ang), (1, 2))
    sin = np.tile(np.sin(ang), (1, 2))
    second = (np.arange(LANES) % ROT_HALF) >= (ROT_HALF // 2)
    sa = np.where(second[None, :], sin, 0.0).astype(np.float32)
    sb = np.where(second[None, :], 0.0, -sin).astype(np.float32)
    return jnp.asarray(cos, F32), jnp.asarray(sa, F32), jnp.asarray(sb, F32)


def _trunk(x3, p, tq, tk):
    batch, seq, _ = x3.shape
    x = x3.reshape(batch * seq, D_MODEL)
    tm = 512
    cos, sa, sb = _rope_tables(seq)
    qm, k2, v2 = _qkv_call(x, p["g_mix0"], p["w_qkv"], p["gq"], p["gk"], cos, sa, sb, seq, tm)
    x, h = _attn_call(qm, k2, v2, x, p["w_o"], p["g_ffn0"], batch, seq, tq, tk)
    x, h = _ffn_call(h, x, p["w_up0"], p["cw0"], p["cb0"], p["w_dn0"], p["g_mix1"], seq, tm, True)
    x, h = _sgu_call(h, x, p["w_in"], p["vg"], p["w_s"], p["b_s"], p["w_out"], p["g_ffn1"], tm)
    x, _ = _ffn_call(h, x, p["w_up1"], p["cw1"], p["cb1"], p["w_dn1"], p["g_ffn1"], seq, tm, False)
    return x.reshape(batch, seq, D_MODEL)


def kernel(x_prompt, x_sample, norm_mix, norm_ffn, attn_w_qkv, attn_q_norm, attn_k_norm, attn_w_o,
           sgu_w_in, sgu_v_norm, sgu_w_s, sgu_b_s, sgu_w_out,
           ffn_w_up, ffn_conv_w, ffn_conv_b, ffn_w_down):
    scale = LOG2_E / math.sqrt(HEAD_DIM)
    p = {
        "g_mix0": norm_mix[0][None, :], "g_mix1": norm_mix[1][None, :],
        "g_ffn0": norm_ffn[0][None, :], "g_ffn1": norm_ffn[1][None, :],
        "w_qkv": attn_w_qkv[0].astype(BF16),
        "gq": jnp.tile(attn_q_norm[0] * scale, 2)[None, :],
        "gk": jnp.tile(attn_k_norm[0], 2)[None, :],
        "w_o": attn_w_o[0].astype(BF16),
        "w_in": sgu_w_in[0].astype(BF16),
        "vg": sgu_v_norm[0][None, :],
        "w_s": sgu_w_s[0].astype(BF16),
        "b_s": jnp.repeat(sgu_b_s[0].T, SGU_GROUP_DIM, axis=1),
        "w_out": sgu_w_out[0].astype(BF16),
        "w_up0": ffn_w_up[0].astype(BF16), "w_up1": ffn_w_up[1].astype(BF16),
        "cw0": ffn_conv_w[0], "cw1": ffn_conv_w[1],
        "cb0": ffn_conv_b[0][None, :], "cb1": ffn_conv_b[1][None, :],
        "w_dn0": ffn_w_down[0].astype(BF16), "w_dn1": ffn_w_down[1].astype(BF16),
    }
    y_prompt = _trunk(x_prompt, p, 512, 2048)
    y_sample = _trunk(x_sample, p, 512, 2048)
    return (y_prompt, y_sample)
```

```python
import functools
import math

import jax
import jax.numpy as jnp
import numpy as np
from jax import lax
from jax.experimental import pallas as pl
from jax.experimental.pallas import tpu as pltpu

D_MODEL = 1024
N_HEADS = 16
HEAD_DIM = 64
N_KV_HEADS = 4
Q_PER_KV = N_HEADS // N_KV_HEADS
ROT_HALF = HEAD_DIM // 2
ROPE_THETA = 10000.0
GRID_W = 64
SGU_CHUNK = 128
SGU_INNER = 2 * D_MODEL
SGU_GROUPS = 8
SGU_GROUP_DIM = SGU_INNER // SGU_GROUPS
FFN_DIM = 2816
EPS = 1e-6
LOG2_E = math.log2(math.e)

LANES = 128
BF16_SUBLANES = 16
VMEM_LIMIT_BYTES = 56 * 1024 * 1024
N_PAIRS = N_HEADS // 2
N_KV_PAIRS = N_KV_HEADS // 2
QKV_OUT = (N_HEADS + 2 * N_KV_HEADS) * HEAD_DIM
K_OFF = N_HEADS * HEAD_DIM
V_OFF = K_OFF + N_KV_HEADS * HEAD_DIM
PROJ_TILE = 2 * LANES
FFN_CHUNK = 256
N_FFN_CHUNKS = FFN_DIM // FFN_CHUNK

F32 = jnp.float32
BF16 = jnp.bfloat16


def _cparams(*sem, n_in=0, fuse=()):
    fusion = [i in fuse for i in range(n_in)] if fuse else None
    return pltpu.CompilerParams(dimension_semantics=sem, vmem_limit_bytes=VMEM_LIMIT_BYTES,
                                allow_input_fusion=fusion)


def _const_spec(shape):
    nd = len(shape)
    return pl.BlockSpec(shape, lambda *_: (0,) * nd, pipeline_mode=pl.Buffered(1))


def _rmsnorm(x, g):
    y = x * lax.rsqrt(jnp.mean(x * x, axis=-1, keepdims=True) + EPS)
    return y * g


def _qkv_kernel(x_ref, g_ref, w_ref, gq_ref, gk_ref, cos_ref, sa_ref, sb_ref,
                qm_ref, k2_ref, v2_ref):
    x = x_ref[...]
    h = _rmsnorm(x, g_ref[...]).astype(BF16)
    tm = x.shape[0]
    lane = lax.broadcasted_iota(jnp.int32, (tm, LANES), 1)
    lo = lane < HEAD_DIM
    cos = cos_ref[...]
    sa = sa_ref[...]
    sb = sb_ref[...]

    row = lax.broadcasted_iota(jnp.int32, (2 * LANES, LANES), 0)
    col = lax.broadcasted_iota(jnp.int32, (2 * LANES, LANES), 1)
    avg = jnp.where((row % LANES) // HEAD_DIM == col // HEAD_DIM, 1.0 / HEAD_DIM, 0.0).astype(BF16)

    def norm_rope(blk, gain):
        sq = blk * blk
        hi = sq.astype(BF16)
        rem = (sq - hi.astype(F32)).astype(BF16)
        ms = jnp.dot(jnp.concatenate([hi, rem], axis=1), avg, preferred_element_type=F32)
        xn = blk * lax.rsqrt(ms + EPS) * gain
        return xn * cos + pltpu.roll(xn, 16, 1) * sa + pltpu.roll(xn, LANES - 16, 1) * sb

    gq = gq_ref[...]
    gk = gk_ref[...]
    n_tiles = QKV_OUT // PROJ_TILE

    def zero_after(v):
        return jnp.minimum(jnp.maximum(v, 0.0), 0.0).astype(BF16)

    def proj(t, dep):
        lhs = h if dep is None else jnp.concatenate([h[:, :LANES] + dep, h[:, LANES:]], axis=1)
        return jnp.dot(lhs, w_ref[:, t * PROJ_TILE:(t + 1) * PROJ_TILE], preferred_element_type=F32)

    def post(t, y):
        last = None
        for half in range(PROJ_TILE // LANES):
            j = t * (PROJ_TILE // LANES) + half
            blk = y[:, LANES * half:LANES * (half + 1)]
            if j < N_PAIRS:
                ro = norm_rope(blk, gq)
                qm_ref[2 * j] = jnp.where(lo, ro, 0.0).astype(BF16)
                qm_ref[2 * j + 1] = jnp.where(lo, 0.0, ro).astype(BF16)
                last = ro
            elif j < N_PAIRS + N_KV_PAIRS:
                jk = j - N_PAIRS
                ro = norm_rope(blk, gk)
                sw = pltpu.roll(ro, HEAD_DIM, 1)
                k2_ref[2 * jk] = jnp.where(lo, ro, sw).astype(BF16)
                k2_ref[2 * jk + 1] = jnp.where(lo, sw, ro).astype(BF16)
                last = ro
            else:
                jv = j - N_PAIRS - N_KV_PAIRS
                sw = pltpu.roll(blk, HEAD_DIM, 1)
                v2_ref[4 * jv] = jnp.where(lo, blk, 1.0).astype(BF16)
                v2_ref[4 * jv + 1] = jnp.where(lo, 1.0, sw).astype(BF16)
                v2_ref[4 * jv + 2] = jnp.where(lo, sw, 1.0).astype(BF16)
                v2_ref[4 * jv + 3] = jnp.where(lo, 1.0, blk).astype(BF16)
                last = sw
        return last

    ys = {0: proj(0, None), 1: proj(1, None)}
    for t in range(n_tiles):
        last = post(t, ys.pop(t))
        if t + 2 < n_tiles:
            ys[t + 2] = proj(t + 2, zero_after(last))


def _qkv_call(x, g, w, gq, gk, cos, sa, sb, seq, tm):
    T = x.shape[0]
    tiles_per_seq = seq // tm
    tab_spec = pl.BlockSpec((tm, LANES), lambda i: (i % tiles_per_seq, 0))
    return pl.pallas_call(
        _qkv_kernel,
        grid=(T // tm,),
        in_specs=[
            pl.BlockSpec((tm, D_MODEL), lambda i: (i, 0)),
            _const_spec((1, D_MODEL)),
            _const_spec((D_MODEL, QKV_OUT)),
            _const_spec((1, LANES)),
            _const_spec((1, LANES)),
            tab_spec, tab_spec, tab_spec,
        ],
        out_specs=[
            pl.BlockSpec((N_HEADS, tm, LANES), lambda i: (0, i, 0)),
            pl.BlockSpec((N_KV_HEADS, tm, LANES), lambda i: (0, i, 0)),
            pl.BlockSpec((2 * N_KV_HEADS, tm, LANES), lambda i: (0, i, 0)),
        ],
        out_shape=[
            jax.ShapeDtypeStruct((N_HEADS, T, LANES), BF16),
            jax.ShapeDtypeStruct((N_KV_HEADS, T, LANES), BF16),
            jax.ShapeDtypeStruct((2 * N_KV_HEADS, T, LANES), BF16),
        ],
        compiler_params=_cparams("parallel", n_in=8, fuse=(2,)),
        name="qkv",
    )(x, g, w, gq, gk, cos, sa, sb)


def _attn_kernel(nk, q_ref, k_ref, v_ref, x_ref, wo_ref, g_ref, xo_ref, h_ref, *scratch):
    j = pl.program_id(2)
    tq = q_ref.shape[1]
    tk = k_ref.shape[1]
    single = nk == 1
    if not single:
        m_ref, acc_ref = scratch

        @pl.when(j == 0)
        def _():
            m_ref[...] = jnp.full(m_ref.shape, -jnp.inf, F32)
            acc_ref[...] = jnp.zeros(acc_ref.shape, F32)

    def scores(h):
        return lax.dot_general(q_ref[h], k_ref[h // Q_PER_KV], (((1,), (1,)), ((), ())),
                               preferred_element_type=F32)

    def softmax_pv(h, s):
        m_cur = s[:, :LANES]
        for c in range(1, tk // LANES):
            m_cur = jnp.maximum(m_cur, s[:, c * LANES:(c + 1) * LANES])
        row_max = jnp.max(m_cur, axis=1, keepdims=True)
        if single:
            m_next = jnp.broadcast_to(row_max, (tq, LANES))
        else:
            m_prev = m_ref[h]
            m_next = jnp.maximum(m_prev, row_max)
        p = jnp.concatenate(
            [jnp.exp2(s[:, c * LANES:(c + 1) * LANES] - m_next).astype(BF16)
             for c in range(tk // LANES)], axis=1)
        v = v_ref[2 * (h // Q_PER_KV) + (h % 2)]
        pv = jnp.dot(p, v, preferred_element_type=F32)
        if single:
            return pv
        alpha = jnp.exp2(m_prev - m_next)
        m_ref[h] = m_next
        acc_ref[h] = alpha * acc_ref[h] + pv
        return None

    accs = []
    s_cur = scores(0)
    for h in range(N_HEADS):
        s_next = scores(h + 1) if h + 1 < N_HEADS else None
        accs.append(softmax_pv(h, s_cur))
        s_cur = s_next

    def epilogue():
        lane = lax.broadcasted_iota(jnp.int32, (tq, LANES), 1)
        lo = lane < HEAD_DIM
        pairs = []
        for pr in range(N_PAIRS):
            a0 = accs[2 * pr] if single else acc_ref[2 * pr]
            a1 = accs[2 * pr + 1] if single else acc_ref[2 * pr + 1]
            num = jnp.where(lo, a0, a1)
            den = pltpu.roll(jnp.where(lo, a1, a0), HEAD_DIM, 1)
            pairs.append((num / den).astype(BF16))
        o = jnp.concatenate(pairs, axis=1)
        x1 = x_ref[...] + jnp.dot(o, wo_ref[...], preferred_element_type=F32)
        xo_ref[...] = x1
        h_ref[...] = _rmsnorm(x1, g_ref[...]).astype(BF16)

    if single:
        epilogue()
    else:
        pl.when(j == nk - 1)(epilogue)


def _attn_call(qm, k2, v2, x, wo, g, batch, seq, tq, tk):
    T = qm.shape[1]
    nq = seq // tq
    nk = seq // tk
    row_spec = pl.BlockSpec((tq, D_MODEL), lambda b, i, j: (b * nq + i, 0))
    return pl.pallas_call(
        functools.partial(_attn_kernel, nk),
        grid=(batch, nq, nk),
        in_specs=[
            pl.BlockSpec((N_HEADS, tq, LANES), lambda b, i, j: (0, b * nq + i, 0)),
            pl.BlockSpec((N_KV_HEADS, tk, LANES), lambda b, i, j: (0, b * nk + j, 0)),
            pl.BlockSpec((2 * N_KV_HEADS, tk, LANES), lambda b, i, j: (0, b * nk + j, 0)),
            row_spec,
            _const_spec((D_MODEL, D_MODEL)),
            _const_spec((1, D_MODEL)),
        ],
        out_specs=[row_spec, row_spec],
        out_shape=[
            jax.ShapeDtypeStruct((T, D_MODEL), F32),
            jax.ShapeDtypeStruct((T, D_MODEL), BF16),
        ],
        scratch_shapes=[] if nk == 1 else [
            pltpu.VMEM((N_HEADS, tq, LANES), F32),
            pltpu.VMEM((N_HEADS, tq, LANES), F32),
        ],
        compiler_params=_cparams("parallel", "parallel", "arbitrary", n_in=6, fuse=(4,)),
        name="attn",
    )(qm, k2, v2, x, wo, g)


def _ffn_kernel(tiles_per_seq, with_norm, hp_ref, h_ref, hn_ref, x_ref, wup_ref, cw_ref, cb_ref,
                wdn_ref, g_ref, xo_ref, *rest):
    i = pl.program_id(0)
    tm = h_ref.shape[0]
    halo = hp_ref.shape[0]
    pos = i % tiles_per_seq
    hp = hp_ref[...]
    hn = hn_ref[...]
    hp = jnp.where(pos == 0, jnp.zeros_like(hp), hp)
    hn = jnp.where(pos == tiles_per_seq - 1, jnp.zeros_like(hn), hn)
    hcat = jnp.concatenate([hp, h_ref[...], hn], axis=0)
    rows = tm + 2 * halo

    def up_proj(c0):
        return jnp.dot(hcat, wup_ref[:, c0:c0 + FFN_CHUNK], preferred_element_type=F32)

    def conv(y, c0):
        prev = pltpu.roll(y, 1, 0)[halo:halo + tm]
        nxt = pltpu.roll(y, rows - 1, 0)[halo:halo + tm]
        w = cw_ref[:, c0:c0 + FFN_CHUNK]
        return (prev * w[0:1] + y[halo:halo + tm] * w[1:2] + nxt * w[2:3]
                + cb_ref[:, c0:c0 + FFN_CHUNK])

    acts = []
    y_gate = up_proj(0)
    y_up = up_proj(FFN_DIM)
    for c in range(N_FFN_CHUNKS):
        if c + 1 < N_FFN_CHUNKS:
            y_gate_next = up_proj((c + 1) * FFN_CHUNK)
            y_up_next = up_proj(FFN_DIM + (c + 1) * FFN_CHUNK)
        gate = conv(y_gate, c * FFN_CHUNK)
        up = conv(y_up, FFN_DIM + c * FFN_CHUNK)
        y_gate, y_up = y_gate_next, y_up_next
        acts.append((gate * (1.0 / (1.0 + jnp.exp2(gate * (-LOG2_E)))) * up).astype(BF16))
    act = jnp.concatenate(acts, axis=1)
    x2 = x_ref[...] + jnp.dot(act, wdn_ref[...], preferred_element_type=F32)
    xo_ref[...] = x2
    if with_norm:
        rest[0][...] = _rmsnorm(x2, g_ref[...]).astype(BF16)


def _ffn_call(h, x, wup, cw, cb, wdn, g, seq, tm, with_norm):
    T = x.shape[0]
    halo = BF16_SUBLANES
    nblk = tm // halo
    last_blk = T // halo - 1
    out_specs = [pl.BlockSpec((tm, D_MODEL), lambda i: (i, 0))]
    out_shape = [jax.ShapeDtypeStruct((T, D_MODEL), F32)]
    if with_norm:
        out_specs.append(pl.BlockSpec((tm, D_MODEL), lambda i: (i, 0)))
        out_shape.append(jax.ShapeDtypeStruct((T, D_MODEL), BF16))
    res = pl.pallas_call(
        functools.partial(_ffn_kernel, seq // tm, with_norm),
        grid=(T // tm,),
        in_specs=[
            pl.BlockSpec((halo, D_MODEL), lambda i: (jnp.maximum(i * nblk - 1, 0), 0)),
            pl.BlockSpec((tm, D_MODEL), lambda i: (i, 0)),
            pl.BlockSpec((halo, D_MODEL), lambda i: (jnp.minimum((i + 1) * nblk, last_blk), 0)),
            pl.BlockSpec((tm, D_MODEL), lambda i: (i, 0)),
            _const_spec((D_MODEL, 2 * FFN_DIM)),
            _const_spec((3, 2 * FFN_DIM)),
            _const_spec((1, 2 * FFN_DIM)),
            _const_spec((FFN_DIM, D_MODEL)),
            _const_spec((1, D_MODEL)),
        ],
        out_specs=out_specs,
        out_shape=out_shape,
        compiler_params=_cparams("parallel", n_in=9, fuse=(4, 7)),
        name="ffn",
    )(h, h, h, x, wup, cw, cb, wdn, g)
    return res if with_norm else (res[0], None)


def _gelu(x):
    c = math.sqrt(2.0 / math.pi)
    return 0.5 * x * (1.0 + jnp.tanh(c * (x + 0.044715 * (x * x * x))))


def _sgu_kernel(h_ref, x_ref, win_ref, vg_ref, ws_ref, bs_ref, wout_ref, g_ref, xo_ref, ho_ref):
    tm = h_ref.shape[0]
    h = h_ref[...]
    v = _gelu(jnp.dot(h, win_ref[:, SGU_INNER:], preferred_element_type=F32))
    v = _rmsnorm(v, vg_ref[...]).astype(BF16)
    ys = []

    def u_proj(g):
        return jnp.dot(h, win_ref[:, g * SGU_GROUP_DIM:(g + 1) * SGU_GROUP_DIM],
                       preferred_element_type=F32)

    u_pre = u_proj(0)
    for g in range(SGU_GROUPS):
        c0 = g * SGU_GROUP_DIM
        u_pre_next = u_proj(g + 1) if g + 1 < SGU_GROUPS else None
        u = _gelu(u_pre)
        u_pre = u_pre_next
        ws = ws_ref[g]
        bias = bs_ref[:, c0:c0 + SGU_GROUP_DIM]
        ss = []
        for c in range(tm // SGU_CHUNK):
            vc = v[c * SGU_CHUNK:(c + 1) * SGU_CHUNK, c0:c0 + SGU_GROUP_DIM]
            ss.append(jnp.dot(ws, vc, preferred_element_type=F32) + bias)
        ys.append((u * jnp.concatenate(ss, axis=0)).astype(BF16))
    y = jnp.concatenate(ys, axis=1)
    x2 = x_ref[...] + jnp.dot(y, wout_ref[...], preferred_element_type=F32)
    xo_ref[...] = x2
    ho_ref[...] = _rmsnorm(x2, g_ref[...]).astype(BF16)


def _sgu_call(h, x, win, vg, ws, bs, wout, g, tm):
    T = x.shape[0]
    return pl.pallas_call(
        _sgu_kernel,
        grid=(T // tm,),
        in_specs=[
            pl.BlockSpec((tm, D_MODEL), lambda i: (i, 0)),
            pl.BlockSpec((tm, D_MODEL), lambda i: (i, 0)),
            _const_spec((D_MODEL, 2 * SGU_INNER)),
            _const_spec((1, SGU_INNER)),
            _const_spec((SGU_GROUPS, SGU_CHUNK, SGU_CHUNK)),
            _const_spec((SGU_CHUNK, SGU_INNER)),
            _const_spec((SGU_INNER, D_MODEL)),
            _const_spec((1, D_MODEL)),
        ],
        out_specs=[
            pl.BlockSpec((tm, D_MODEL), lambda i: (i, 0)),
            pl.BlockSpec((tm, D_MODEL), lambda i: (i, 0)),
        ],
        out_shape=[
            jax.ShapeDtypeStruct((T, D_MODEL), F32),
            jax.ShapeDtypeStruct((T, D_MODEL), BF16),
        ],
        compiler_params=_cparams("parallel", n_in=8, fuse=(2, 4, 6)),
        name="sgu",
    )(h, x, win, vg, ws, bs, wout, g)


def _rope_tables(seq):
    rows = seq // GRID_W
    row_idx = np.repeat(np.arange(rows), GRID_W).astype(np.float32)
    col_idx = np.tile(np.arange(GRID_W), rows).astype(np.float32)
    inv_freq = (np.float32(ROPE_THETA)
                ** (-np.arange(0, ROT_HALF, 2, dtype=np.float32) / np.float32(ROT_HALF)))
    ang_r = row_idx[:, None] * inv_freq[None, :]
    ang_c = col_idx[:, None] * inv_freq[None, :]
    ang = np.concatenate([ang_r, ang_r, ang_c, ang_c], axis=-1)
    cos = np.tile(np.cos(ang), (1, 2))
    sin = np.tile(np.sin(ang), (1, 2))
    second = (np.arange(LANES) % ROT_HALF) >= (ROT_HALF // 2)
    sa = np.where(second[None, :], sin, 0.0).astype(np.float32)
    sb = np.where(second[None, :], 0.0, -sin).astype(np.float32)
    return jnp.asarray(cos, F32), jnp.asarray(sa, F32), jnp.asarray(sb, F32)


def _trunk(x3, p, tq, tk):
    batch, seq, _ = x3.shape
    x = x3.reshape(batch * seq, D_MODEL)
    tm = 512
    cos, sa, sb = _rope_tables(seq)
    qm, k2, v2 = _qkv_call(x, p["g_mix0"], p["w_qkv"], p["gq"], p["gk"], cos, sa, sb, seq, tm)
    x, h = _attn_call(qm, k2, v2, x, p["w_o"], p["g_ffn0"], batch, seq, tq, tk)
    x, h = _ffn_call(h, x, p["w_up0"], p["cw0"], p["cb0"], p["w_dn0"], p["g_mix1"], seq, tm, True)
    x, h = _sgu_call(h, x, p["w_in"], p["vg"], p["w_s"], p["b_s"], p["w_out"], p["g_ffn1"], tm)
    x, _ = _ffn_call(h, x, p["w_up1"], p["cw1"], p["cb1"], p["w_dn1"], p["g_ffn1"], seq, tm, False)
    return x.reshape(batch, seq, D_MODEL)


def kernel(x_prompt, x_sample, norm_mix, norm_ffn, attn_w_qkv, attn_q_norm, attn_k_norm, attn_w_o,
           sgu_w_in, sgu_v_norm, sgu_w_s, sgu_b_s, sgu_w_out,
           ffn_w_up, ffn_conv_w, ffn_conv_b, ffn_w_down):
    scale = LOG2_E / math.sqrt(HEAD_DIM)
    p = {
        "g_mix0": norm_mix[0][None, :], "g_mix1": norm_mix[1][None, :],
        "g_ffn0": norm_ffn[0][None, :], "g_ffn1": norm_ffn[1][None, :],
        "w_qkv": attn_w_qkv[0].astype(BF16),
        "gq": jnp.tile(attn_q_norm[0] * scale, 2)[None, :],
        "gk": jnp.tile(attn_k_norm[0], 2)[None, :],
        "w_o": attn_w_o[0].astype(BF16),
        "w_in": sgu_w_in[0].astype(BF16),
        "vg": sgu_v_norm[0][None, :],
        "w_s": sgu_w_s[0].astype(BF16),
        "b_s": jnp.repeat(sgu_b_s[0].T, SGU_GROUP_DIM, axis=1),
        "w_out": sgu_w_out[0].astype(BF16),
        "w_up0": ffn_w_up[0].astype(BF16), "w_up1": ffn_w_up[1].astype(BF16),
        "cw0": ffn_conv_w[0], "cw1": ffn_conv_w[1],
        "cb0": ffn_conv_b[0][None, :], "cb1": ffn_conv_b[1][None, :],
        "w_dn0": ffn_w_down[0].astype(BF16), "w_dn1": ffn_w_down[1].astype(BF16),
    }
    y_prompt = _trunk(x_prompt, p, 512, 2048)
    y_sample = _trunk(x_sample, p, 512, 2048)
    return (y_prompt, y_sample)
```

```python
import functools
import math

import jax
import jax.numpy as jnp
import numpy as np
from jax import lax
from jax.experimental import pallas as pl
from jax.experimental.pallas import tpu as pltpu

D_MODEL = 1024
N_HEADS = 16
HEAD_DIM = 64
N_KV_HEADS = 4
Q_PER_KV = N_HEADS // N_KV_HEADS
ROT_HALF = HEAD_DIM // 2
ROPE_THETA = 10000.0
GRID_W = 64
SGU_CHUNK = 128
SGU_INNER = 2 * D_MODEL
SGU_GROUPS = 8
SGU_GROUP_DIM = SGU_INNER // SGU_GROUPS
FFN_DIM = 2816
EPS = 1e-6
LOG2_E = math.log2(math.e)

LANES = 128
BF16_SUBLANES = 16
VMEM_LIMIT_BYTES = 56 * 1024 * 1024
N_PAIRS = N_HEADS // 2
N_KV_PAIRS = N_KV_HEADS // 2
QKV_OUT = (N_HEADS + 2 * N_KV_HEADS) * HEAD_DIM
K_OFF = N_HEADS * HEAD_DIM
V_OFF = K_OFF + N_KV_HEADS * HEAD_DIM
PROJ_TILE = 2 * LANES
FFN_CHUNK = 256
N_FFN_CHUNKS = FFN_DIM // FFN_CHUNK

F32 = jnp.float32
BF16 = jnp.bfloat16


def _cparams(*sem):
    return pltpu.CompilerParams(dimension_semantics=sem, vmem_limit_bytes=VMEM_LIMIT_BYTES)


def _const_spec(shape):
    nd = len(shape)
    return pl.BlockSpec(shape, lambda *_: (0,) * nd, pipeline_mode=pl.Buffered(1))


def _rmsnorm(x, g):
    y = x * lax.rsqrt(jnp.mean(x * x, axis=-1, keepdims=True) + EPS)
    return y * g


def _qkv_kernel(x_ref, g_ref, w_ref, gq_ref, gk_ref, cos_ref, sa_ref, sb_ref,
                qm_ref, k2_ref, v2_ref):
    x = x_ref[...]
    h = _rmsnorm(x, g_ref[...]).astype(BF16)
    tm = x.shape[0]
    lane = lax.broadcasted_iota(jnp.int32, (tm, LANES), 1)
    lo = lane < HEAD_DIM
    cos = cos_ref[...]
    sa = sa_ref[...]
    sb = sb_ref[...]

    row = lax.broadcasted_iota(jnp.int32, (2 * LANES, LANES), 0)
    col = lax.broadcasted_iota(jnp.int32, (2 * LANES, LANES), 1)
    avg = jnp.where((row % LANES) // HEAD_DIM == col // HEAD_DIM, 1.0 / HEAD_DIM, 0.0).astype(BF16)

    def norm_rope(blk, gain):
        sq = blk * blk
        hi = sq.astype(BF16)
        rem = (sq - hi.astype(F32)).astype(BF16)
        ms = jnp.dot(jnp.concatenate([hi, rem], axis=1), avg, preferred_element_type=F32)
        xn = blk * lax.rsqrt(ms + EPS) * gain
        return xn * cos + pltpu.roll(xn, 16, 1) * sa + pltpu.roll(xn, LANES - 16, 1) * sb

    gq = gq_ref[...]
    gk = gk_ref[...]
    n_tiles = QKV_OUT // PROJ_TILE

    def zero_after(v):
        return jnp.minimum(jnp.maximum(v, 0.0), 0.0).astype(BF16)

    def proj(t, dep):
        lhs = h if dep is None else jnp.concatenate([h[:, :LANES] + dep, h[:, LANES:]], axis=1)
        return jnp.dot(lhs, w_ref[:, t * PROJ_TILE:(t + 1) * PROJ_TILE], preferred_element_type=F32)

    def post(t, y):
        last = None
        for half in range(PROJ_TILE // LANES):
            j = t * (PROJ_TILE // LANES) + half
            blk = y[:, LANES * half:LANES * (half + 1)]
            if j < N_PAIRS:
                ro = norm_rope(blk, gq)
                qm_ref[2 * j] = jnp.where(lo, ro, 0.0).astype(BF16)
                qm_ref[2 * j + 1] = jnp.where(lo, 0.0, ro).astype(BF16)
                last = ro
            elif j < N_PAIRS + N_KV_PAIRS:
                jk = j - N_PAIRS
                ro = norm_rope(blk, gk)
                sw = pltpu.roll(ro, HEAD_DIM, 1)
                k2_ref[2 * jk] = jnp.where(lo, ro, sw).astype(BF16)
                k2_ref[2 * jk + 1] = jnp.where(lo, sw, ro).astype(BF16)
                last = ro
            else:
                jv = j - N_PAIRS - N_KV_PAIRS
                sw = pltpu.roll(blk, HEAD_DIM, 1)
                v2_ref[4 * jv] = jnp.where(lo, blk, 1.0).astype(BF16)
                v2_ref[4 * jv + 1] = jnp.where(lo, 1.0, sw).astype(BF16)
                v2_ref[4 * jv + 2] = jnp.where(lo, sw, 1.0).astype(BF16)
                v2_ref[4 * jv + 3] = jnp.where(lo, 1.0, blk).astype(BF16)
                last = sw
        return last

    ys = {0: proj(0, None), 1: proj(1, None)}
    for t in range(n_tiles):
        last = post(t, ys.pop(t))
        if t + 2 < n_tiles:
            ys[t + 2] = proj(t + 2, zero_after(last))


def _qkv_call(x, g, w, gq, gk, cos, sa, sb, seq, tm):
    T = x.shape[0]
    tiles_per_seq = seq // tm
    tab_spec = pl.BlockSpec((tm, LANES), lambda i: (i % tiles_per_seq, 0))
    return pl.pallas_call(
        _qkv_kernel,
        grid=(T // tm,),
        in_specs=[
            pl.BlockSpec((tm, D_MODEL), lambda i: (i, 0)),
            _const_spec((1, D_MODEL)),
            _const_spec((D_MODEL, QKV_OUT)),
            _const_spec((1, LANES)),
            _const_spec((1, LANES)),
            tab_spec, tab_spec, tab_spec,
        ],
        out_specs=[
            pl.BlockSpec((N_HEADS, tm, LANES), lambda i: (0, i, 0)),
            pl.BlockSpec((N_KV_HEADS, tm, LANES), lambda i: (0, i, 0)),
            pl.BlockSpec((2 * N_KV_HEADS, tm, LANES), lambda i: (0, i, 0)),
        ],
        out_shape=[
            jax.ShapeDtypeStruct((N_HEADS, T, LANES), BF16),
            jax.ShapeDtypeStruct((N_KV_HEADS, T, LANES), BF16),
            jax.ShapeDtypeStruct((2 * N_KV_HEADS, T, LANES), BF16),
        ],
        compiler_params=_cparams("parallel"),
        name="qkv",
    )(x, g, w, gq, gk, cos, sa, sb)


def _attn_kernel(nk, q_ref, k_ref, v_ref, x_ref, wo_ref, g_ref, xo_ref, h_ref, *scratch):
    j = pl.program_id(2)
    tq = q_ref.shape[1]
    tk = k_ref.shape[1]
    single = nk == 1
    if not single:
        m_ref, acc_ref = scratch

        @pl.when(j == 0)
        def _():
            m_ref[...] = jnp.full(m_ref.shape, -jnp.inf, F32)
            acc_ref[...] = jnp.zeros(acc_ref.shape, F32)

    def scores(h):
        return lax.dot_general(q_ref[h], k_ref[h // Q_PER_KV], (((1,), (1,)), ((), ())),
                               preferred_element_type=F32)

    def softmax_pv(h, s):
        m_cur = s[:, :LANES]
        for c in range(1, tk // LANES):
            m_cur = jnp.maximum(m_cur, s[:, c * LANES:(c + 1) * LANES])
        row_max = jnp.max(m_cur, axis=1, keepdims=True)
        if single:
            m_next = jnp.broadcast_to(row_max, (tq, LANES))
        else:
            m_prev = m_ref[h]
            m_next = jnp.maximum(m_prev, row_max)
        p = jnp.concatenate(
            [jnp.exp2(s[:, c * LANES:(c + 1) * LANES] - m_next).astype(BF16)
             for c in range(tk // LANES)], axis=1)
        v = v_ref[2 * (h // Q_PER_KV) + (h % 2)]
        pv = jnp.dot(p, v, preferred_element_type=F32)
        if single:
            return pv
        alpha = jnp.exp2(m_prev - m_next)
        m_ref[h] = m_next
        acc_ref[h] = alpha * acc_ref[h] + pv
        return None

    accs = []
    s_cur = scores(0)
    for h in range(N_HEADS):
        s_next = scores(h + 1) if h + 1 < N_HEADS else None
        accs.append(softmax_pv(h, s_cur))
        s_cur = s_next

    def epilogue():
        lane = lax.broadcasted_iota(jnp.int32, (tq, LANES), 1)
        lo = lane < HEAD_DIM
        pairs = []
        for pr in range(N_PAIRS):
            a0 = accs[2 * pr] if single else acc_ref[2 * pr]
            a1 = accs[2 * pr + 1] if single else acc_ref[2 * pr + 1]
            num = jnp.where(lo, a0, a1)
            den = pltpu.roll(jnp.where(lo, a1, a0), HEAD_DIM, 1)
            pairs.append((num / den).astype(BF16))
        o = jnp.concatenate(pairs, axis=1)
        x1 = x_ref[...] + jnp.dot(o, wo_ref[...], preferred_element_type=F32)
        xo_ref[...] = x1
        h_ref[...] = _rmsnorm(x1, g_ref[...]).astype(BF16)

    if single:
        epilogue()
    else:
        pl.when(j == nk - 1)(epilogue)


def _attn_call(qm, k2, v2, x, wo, g, batch, seq, tq, tk):
    T = qm.shape[1]
    nq = seq // tq
    nk = seq // tk
    row_spec = pl.BlockSpec((tq, D_MODEL), lambda b, i, j: (b * nq + i, 0))
    return pl.pallas_call(
        functools.partial(_attn_kernel, nk),
        grid=(batch, nq, nk),
        in_specs=[
            pl.BlockSpec((N_HEADS, tq, LANES), lambda b, i, j: (0, b * nq + i, 0)),
            pl.BlockSpec((N_KV_HEADS, tk, LANES), lambda b, i, j: (0, b * nk + j, 0)),
            pl.BlockSpec((2 * N_KV_HEADS, tk, LANES), lambda b, i, j: (0, b * nk + j, 0)),
            row_spec,
            _const_spec((D_MODEL, D_MODEL)),
            _const_spec((1, D_MODEL)),
        ],
        out_specs=[row_spec, row_spec],
        out_shape=[
            jax.ShapeDtypeStruct((T, D_MODEL), F32),
            jax.ShapeDtypeStruct((T, D_MODEL), BF16),
        ],
        scratch_shapes=[] if nk == 1 else [
            pltpu.VMEM((N_HEADS, tq, LANES), F32),
            pltpu.VMEM((N_HEADS, tq, LANES), F32),
        ],
        compiler_params=_cparams("parallel", "parallel", "arbitrary"),
        name="attn",
    )(qm, k2, v2, x, wo, g)


def _ffn_kernel(tiles_per_seq, with_norm, hp_ref, h_ref, hn_ref, x_ref, wup_ref, cw_ref, cb_ref,
                wdn_ref, g_ref, xo_ref, *rest):
    i = pl.program_id(0)
    tm = h_ref.shape[0]
    halo = hp_ref.shape[0]
    pos = i % tiles_per_seq
    hp = hp_ref[...]
    hn = hn_ref[...]
    hp = jnp.where(pos == 0, jnp.zeros_like(hp), hp)
    hn = jnp.where(pos == tiles_per_seq - 1, jnp.zeros_like(hn), hn)
    hcat = jnp.concatenate([hp, h_ref[...], hn], axis=0)
    rows = tm + 2 * halo

    def up_proj(c0):
        return jnp.dot(hcat, wup_ref[:, c0:c0 + FFN_CHUNK], preferred_element_type=F32)

    def conv(y, c0):
        prev = pltpu.roll(y, 1, 0)[halo:halo + tm]
        nxt = pltpu.roll(y, rows - 1, 0)[halo:halo + tm]
        w = cw_ref[:, c0:c0 + FFN_CHUNK]
        return (prev * w[0:1] + y[halo:halo + tm] * w[1:2] + nxt * w[2:3]
                + cb_ref[:, c0:c0 + FFN_CHUNK])

    acts = []
    for c in range(N_FFN_CHUNKS):
        gate = conv(up_proj(c * FFN_CHUNK), c * FFN_CHUNK)
        up = conv(up_proj(FFN_DIM + c * FFN_CHUNK), FFN_DIM + c * FFN_CHUNK)
        acts.append((gate * (1.0 / (1.0 + jnp.exp2(gate * (-LOG2_E)))) * up).astype(BF16))
    act = jnp.concatenate(acts, axis=1)
    x2 = x_ref[...] + jnp.dot(act, wdn_ref[...], preferred_element_type=F32)
    xo_ref[...] = x2
    if with_norm:
        rest[0][...] = _rmsnorm(x2, g_ref[...]).astype(BF16)


def _ffn_call(h, x, wup, cw, cb, wdn, g, seq, tm, with_norm):
    T = x.shape[0]
    halo = BF16_SUBLANES
    nblk = tm // halo
    last_blk = T // halo - 1
    out_specs = [pl.BlockSpec((tm, D_MODEL), lambda i: (i, 0))]
    out_shape = [jax.ShapeDtypeStruct((T, D_MODEL), F32)]
    if with_norm:
        out_specs.append(pl.BlockSpec((tm, D_MODEL), lambda i: (i, 0)))
        out_shape.append(jax.ShapeDtypeStruct((T, D_MODEL), BF16))
    res = pl.pallas_call(
        functools.partial(_ffn_kernel, seq // tm, with_norm),
        grid=(T // tm,),
        in_specs=[
            pl.BlockSpec((halo, D_MODEL), lambda i: (jnp.maximum(i * nblk - 1, 0), 0)),
            pl.BlockSpec((tm, D_MODEL), lambda i: (i, 0)),
            pl.BlockSpec((halo, D_MODEL), lambda i: (jnp.minimum((i + 1) * nblk, last_blk), 0)),
            pl.BlockSpec((tm, D_MODEL), lambda i: (i, 0)),
            _const_spec((D_MODEL, 2 * FFN_DIM)),
            _const_spec((3, 2 * FFN_DIM)),
            _const_spec((1, 2 * FFN_DIM)),
            _const_spec((FFN_DIM, D_MODEL)),
            _const_spec((1, D_MODEL)),
        ],
        out_specs=out_specs,
        out_shape=out_shape,
        compiler_params=_cparams("parallel"),
        name="ffn",
    )(h, h, h, x, wup, cw, cb, wdn, g)
    return res if with_norm else (res[0], None)


def _gelu(x):
    c = math.sqrt(2.0 / math.pi)
    return 0.5 * x * (1.0 + jnp.tanh(c * (x + 0.044715 * (x * x * x))))


def _sgu_kernel(h_ref, x_ref, win_ref, vg_ref, ws_ref, bs_ref, wout_ref, g_ref, xo_ref, ho_ref):
    tm = h_ref.shape[0]
    h = h_ref[...]
    v = _gelu(jnp.dot(h, win_ref[:, SGU_INNER:], preferred_element_type=F32))
    v = _rmsnorm(v, vg_ref[...]).astype(BF16)
    ys = []

    def u_proj(g):
        return jnp.dot(h, win_ref[:, g * SGU_GROUP_DIM:(g + 1) * SGU_GROUP_DIM],
                       preferred_element_type=F32)

    for g in range(SGU_GROUPS):
        c0 = g * SGU_GROUP_DIM
        u = _gelu(u_proj(g))
        ws = ws_ref[g]
        bias = bs_ref[:, c0:c0 + SGU_GROUP_DIM]
        ss = []
        for c in range(tm // SGU_CHUNK):
            vc = v[c * SGU_CHUNK:(c + 1) * SGU_CHUNK, c0:c0 + SGU_GROUP_DIM]
            ss.append(jnp.dot(ws, vc, preferred_element_type=F32) + bias)
        ys.append((u * jnp.concatenate(ss, axis=0)).astype(BF16))
    y = jnp.concatenate(ys, axis=1)
    x2 = x_ref[...] + jnp.dot(y, wout_ref[...], preferred_element_type=F32)
    xo_ref[...] = x2
    ho_ref[...] = _rmsnorm(x2, g_ref[...]).astype(BF16)


def _sgu_call(h, x, win, vg, ws, bs, wout, g, tm):
    T = x.shape[0]
    return pl.pallas_call(
        _sgu_kernel,
        grid=(T // tm,),
        in_specs=[
            pl.BlockSpec((tm, D_MODEL), lambda i: (i, 0)),
            pl.BlockSpec((tm, D_MODEL), lambda i: (i, 0)),
            _const_spec((D_MODEL, 2 * SGU_INNER)),
            _const_spec((1, SGU_INNER)),
            _const_spec((SGU_GROUPS, SGU_CHUNK, SGU_CHUNK)),
            _const_spec((SGU_CHUNK, SGU_INNER)),
            _const_spec((SGU_INNER, D_MODEL)),
            _const_spec((1, D_MODEL)),
        ],
        out_specs=[
            pl.BlockSpec((tm, D_MODEL), lambda i: (i, 0)),
            pl.BlockSpec((tm, D_MODEL), lambda i: (i, 0)),
        ],
        out_shape=[
            jax.ShapeDtypeStruct((T, D_MODEL), F32),
            jax.ShapeDtypeStruct((T, D_MODEL), BF16),
        ],
        compiler_params=_cparams("parallel"),
        name="sgu",
    )(h, x, win, vg, ws, bs, wout, g)


def _rope_tables(seq):
    rows = seq // GRID_W
    row_idx = np.repeat(np.arange(rows), GRID_W).astype(np.float32)
    col_idx = np.tile(np.arange(GRID_W), rows).astype(np.float32)
    inv_freq = (np.float32(ROPE_THETA)
                ** (-np.arange(0, ROT_HALF, 2, dtype=np.float32) / np.float32(ROT_HALF)))
    ang_r = row_idx[:, None] * inv_freq[None, :]
    ang_c = col_idx[:, None] * inv_freq[None, :]
    ang = np.concatenate([ang_r, ang_r, ang_c, ang_c], axis=-1)
    cos = np.tile(np.cos(ang), (1, 2))
    sin = np.tile(np.sin(ang), (1, 2))
    second = (np.arange(LANES) % ROT_HALF) >= (ROT_HALF // 2)
    sa = np.where(second[None, :], sin, 0.0).astype(np.float32)
    sb = np.where(second[None, :], 0.0, -sin).astype(np.float32)
    return jnp.asarray(cos, F32), jnp.asarray(sa, F32), jnp.asarray(sb, F32)


def _trunk(x3, p, tq, tk):
    batch, seq, _ = x3.shape
    x = x3.reshape(batch * seq, D_MODEL)
    tm = 512
    cos, sa, sb = _rope_tables(seq)
    qm, k2, v2 = _qkv_call(x, p["g_mix0"], p["w_qkv"], p["gq"], p["gk"], cos, sa, sb, seq, tm)
    x, h = _attn_call(qm, k2, v2, x, p["w_o"], p["g_ffn0"], batch, seq, tq, tk)
    x, h = _ffn_call(h, x, p["w_up0"], p["cw0"], p["cb0"], p["w_dn0"], p["g_mix1"], seq, tm, True)
    x, h = _sgu_call(h, x, p["w_in"], p["vg"], p["w_s"], p["b_s"], p["w_out"], p["g_ffn1"], tm)
    x, _ = _ffn_call(h, x, p["w_up1"], p["cw1"], p["cb1"], p["w_dn1"], p["g_ffn1"], seq, tm, False)
    return x.reshape(batch, seq, D_MODEL)


def kernel(x_prompt, x_sample, norm_mix, norm_ffn, attn_w_qkv, attn_q_norm, attn_k_norm, attn_w_o,
           sgu_w_in, sgu_v_norm, sgu_w_s, sgu_b_s, sgu_w_out,
           ffn_w_up, ffn_conv_w, ffn_conv_b, ffn_w_down):
    scale = LOG2_E / math.sqrt(HEAD_DIM)
    p = {
        "g_mix0": norm_mix[0][None, :], "g_mix1": norm_mix[1][None, :],
        "g_ffn0": norm_ffn[0][None, :], "g_ffn1": norm_ffn[1][None, :],
        "w_qkv": attn_w_qkv[0].astype(BF16),
        "gq": jnp.tile(attn_q_norm[0] * scale, 2)[None, :],
        "gk": jnp.tile(attn_k_norm[0], 2)[None, :],
        "w_o": attn_w_o[0].astype(BF16),
        "w_in": sgu_w_in[0].astype(BF16),
        "vg": sgu_v_norm[0][None, :],
        "w_s": sgu_w_s[0].astype(BF16),
        "b_s": jnp.repeat(sgu_b_s[0].T, SGU_GROUP_DIM, axis=1),
        "w_out": sgu_w_out[0].astype(BF16),
        "w_up0": ffn_w_up[0].astype(BF16), "w_up1": ffn_w_up[1].astype(BF16),
        "cw0": ffn_conv_w[0], "cw1": ffn_conv_w[1],
        "cb0": ffn_conv_b[0][None, :], "cb1": ffn_conv_b[1][None, :],
        "w_dn0": ffn_w_down[0].astype(BF16), "w_dn1": ffn_w_down[1].astype(BF16),
    }
    y_prompt = _trunk(x_prompt, p, 512, 2048)
    y_sample = _trunk(x_sample, p, 512, 2048)
    return (y_prompt, y_sample)
```
